```python
import jax
import jax.numpy as jnp
from jax import lax
import numpy as np

D_MODEL = 2048
BATCH = 8
SEQ = 4096
DEPTH = 4

GDN_HEAD_DIM = 128
GDN_HEADS = D_MODEL // GDN_HEAD_DIM
GDN_WIDTH = GDN_HEADS * GDN_HEAD_DIM
CONV_K = 4
CHUNK = 64
SWA_HEAD_DIM = 64
SWA_Q_HEADS = D_MODEL // SWA_HEAD_DIM
SWA_KV_HEADS = SWA_Q_HEADS // 8
SWA_Q_WIDTH = SWA_Q_HEADS * SWA_HEAD_DIM
SWA_KV_WIDTH = SWA_KV_HEADS * SWA_HEAD_DIM
WINDOW = 128
N_BRANCH = 2
D_FF = 4 * D_MODEL
PROJ_SIZES = (3 * GDN_WIDTH, GDN_WIDTH, GDN_HEADS, GDN_HEADS, SWA_Q_WIDTH, SWA_KV_WIDTH, SWA_KV_WIDTH, N_BRANCH * D_MODEL)
D_IN = 3 * GDN_WIDTH + GDN_WIDTH + 2 * GDN_HEADS + SWA_Q_WIDTH + 2 * SWA_KV_WIDTH + N_BRANCH * D_MODEL
NORM_EPS = 1e-6

kernel_name = "hybrid_gdn_swa_gated_parallel_trunk"


def rms_norm(x, gain):
    xf = x.astype(jnp.float32)
    y = xf * lax.rsqrt(jnp.mean(xf * xf, axis=-1, keepdims=True) + NORM_EPS)
    return (y * gain.astype(jnp.float32)).astype(x.dtype)


def l2_normalize(x):
    xf = x.astype(jnp.float32)
    return xf * lax.rsqrt(jnp.sum(xf * xf, axis=-1, keepdims=True) + NORM_EPS)


def causal_depthwise_conv(x, w):
    return lax.conv_general_dilated(
        x, w[:, None, :].astype(x.dtype), window_strides=(1,), padding=[(w.shape[0] - 1, 0)],
        dimension_numbers=("NWC", "WIO", "NWC"), feature_group_count=x.shape[-1])


def alibi_slopes(n_heads):
    return 2.0 ** (-8.0 * jnp.arange(1, n_heads + 1, dtype=jnp.float32) / n_heads)


def gated_delta_rule(q, k, v, g, beta):
    B, T, H, Dk = q.shape
    Dv = v.shape[-1]
    N = T // CHUNK
    f32 = jnp.float32

    def to_chunks(t):
        return t.astype(f32).reshape(B, N, CHUNK, H, -1).transpose(0, 3, 1, 2, 4)

    q = to_chunks(q) * (Dk ** -0.5)
    k = to_chunks(k)
    v = to_chunks(v)
    g = to_chunks(g[..., None])[..., 0]
    beta = to_chunks(beta[..., None])
    decay = jnp.cumsum(g, axis=-1)
    idx = jnp.arange(CHUNK)
    causal = idx[:, None] >= idx[None, :]
    strict = idx[:, None] > idx[None, :]
    gamma = jnp.exp(jnp.where(causal, decay[..., :, None] - decay[..., None, :], -jnp.inf))
    k_beta = k * beta
    a_low = jnp.where(strict, jnp.einsum("bhnid,bhnjd->bhnij", k_beta, k) * gamma, 0.0)
    rhs = jnp.concatenate([v * beta, k_beta * jnp.exp(decay)[..., None]], axis=-1)
    sol = lax.linalg.triangular_solve(a_low + jnp.eye(CHUNK, dtype=f32), rhs,
                                      left_side=True, lower=True, unit_diagonal=True)
    u, w = sol[..., :Dv], sol[..., Dv:]
    qk = jnp.einsum("bhnid,bhnjd->bhnij", q, k) * gamma
    q_dec = q * jnp.exp(decay)[..., None]
    k_dec = k * jnp.exp(decay[..., -1:] - decay)[..., None]
    chunk_decay = jnp.exp(decay[..., -1])

    def step(state, inp):
        qk_n, q_n, w_n, u_n, k_n, d_n = inp
        v_new = u_n - jnp.einsum("bhcd,bhde->bhce", w_n, state)
        o_n = jnp.einsum("bhcd,bhde->bhce", q_n, state) + jnp.einsum("bhij,bhje->bhie", qk_n, v_new)
        state = state * d_n[..., None, None] + jnp.einsum("bhcd,bhce->bhde", k_n, v_new)
        return state, o_n

    xs = tuple(jnp.moveaxis(t, 2, 0) for t in (qk, q_dec, w, u, k_dec, chunk_decay))
    _, o = lax.scan(step, jnp.zeros((B, H, Dk, Dv), f32), xs)
    return o.transpose(1, 0, 3, 2, 4).reshape(B, T, H, Dv)


def gdn_branch(qkv, z, b_logit, a_logit, conv_w, a_log, dt_bias, norm_g):
    B, T, _ = qkv.shape
    f32 = jnp.float32
    qkv_c = jax.nn.silu(causal_depthwise_conv(qkv, conv_w))
    q, k, v = jnp.split(qkv_c, 3, axis=-1)

    def heads(t):
        return t.reshape(B, T, GDN_HEADS, GDN_HEAD_DIM)

    q = l2_normalize(heads(q))
    k = l2_normalize(heads(k))
    beta = jax.nn.sigmoid(b_logit.astype(f32))
    g = -jnp.exp(a_log.astype(f32)) * jax.nn.softplus(a_logit.astype(f32) + dt_bias.astype(f32))
    o = gated_delta_rule(q, k, heads(v), g, beta)
    o = rms_norm(o, norm_g) * jax.nn.silu(heads(z).astype(f32))
    return o.reshape(B, T, GDN_WIDTH).astype(qkv.dtype)


def sliding_window_gqa(q, k, v, sinks):
    B, T, _ = q.shape
    NB = T // WINDOW
    G = SWA_Q_HEADS // SWA_KV_HEADS
    f32 = jnp.float32
    qb = q.reshape(B, NB, WINDOW, SWA_KV_HEADS, G, SWA_HEAD_DIM)

    def band(t):
        tb = t.reshape(B, NB, WINDOW, SWA_KV_HEADS, SWA_HEAD_DIM)
        prev = jnp.pad(tb, ((0, 0), (1, 0), (0, 0), (0, 0), (0, 0)))[:, :-1]
        return jnp.concatenate([prev, tb], axis=2)

    kb, vb = band(k), band(v)
    scores = jnp.einsum("bnqkgd,bnskd->bnkgqs", qb, kb).astype(f32) * (SWA_HEAD_DIM ** -0.5)
    qi = jnp.arange(WINDOW)[:, None]
    sj = jnp.arange(2 * WINDOW)[None, :]
    dist = qi + WINDOW - sj
    in_window = (dist >= 0) & (dist < WINDOW)
    after_start = (jnp.arange(NB)[:, None, None] > 0) | (sj >= WINDOW)[None]
    valid = in_window[None] & after_start
    slopes = alibi_slopes(SWA_Q_HEADS).reshape(SWA_KV_HEADS, G)
    bias = -slopes[:, :, None, None] * dist.astype(f32)
    scores = jnp.where(valid[None, :, None, None], scores + bias, -jnp.inf)
    sink = sinks.astype(f32).reshape(SWA_KV_HEADS, G)[:, :, None, None]
    m = jnp.maximum(scores.max(axis=-1, keepdims=True), sink)
    p = jnp.exp(scores - m)
    probs = (p / (p.sum(axis=-1, keepdims=True) + jnp.exp(sink - m))).astype(v.dtype)
    o = jnp.einsum("bnkgqs,bnskd->bnqkgd", probs, vb)
    return o.reshape(B, T, SWA_Q_WIDTH)


def _fwd_setup_inputs(seed: int = 0) -> dict:
    key = jax.random.key(seed)
    ks = jax.random.split(key, 16)
    f32 = jnp.float32
    nrm = lambda k, shape, scale: jax.random.normal(k, shape, f32) * scale
    dt = jnp.exp(jax.random.uniform(ks[5], (DEPTH, GDN_HEADS), f32, np.log(1e-3), np.log(1e-1)))
    return {
        "x": jax.random.normal(ks[0], (BATCH, SEQ, D_MODEL), f32),
        "norm1_g": 1.0 + nrm(ks[1], (DEPTH, D_MODEL), 0.02),
        "w_in": nrm(ks[2], (DEPTH, D_MODEL, D_IN), D_MODEL ** -0.5),
        "conv_w": nrm(ks[3], (DEPTH, CONV_K, 3 * GDN_WIDTH), CONV_K ** -0.5),
        "a_log": jnp.log(jax.random.uniform(ks[4], (DEPTH, GDN_HEADS), f32, 1.0, 16.0)),
        "dt_bias": dt + jnp.log(-jnp.expm1(-dt)),
        "gdn_norm_g": 1.0 + nrm(ks[6], (DEPTH, GDN_HEAD_DIM), 0.02),
        "attn_sinks": nrm(ks[7], (DEPTH, SWA_Q_HEADS), 0.5),
        "w_branch_gdn": nrm(ks[8], (DEPTH, GDN_WIDTH, D_MODEL), GDN_WIDTH ** -0.5),
        "w_branch_swa": nrm(ks[9], (DEPTH, SWA_Q_WIDTH, D_MODEL), SWA_Q_WIDTH ** -0.5),
        "w_out": nrm(ks[10], (DEPTH, D_MODEL, D_MODEL), D_MODEL ** -0.5),
        "norm2_g": 1.0 + nrm(ks[11], (DEPTH, D_MODEL), 0.02),
        "w_ff_up": nrm(ks[12], (DEPTH, D_MODEL, D_FF), D_MODEL ** -0.5),
        "w_ff_down": nrm(ks[13], (DEPTH, D_FF, D_MODEL), D_FF ** -0.5),
        "final_norm_g": 1.0 + nrm(ks[14], (D_MODEL,), 0.02),
    }


def _fwd_reference(x, norm1_g, w_in, conv_w, a_log, dt_bias, gdn_norm_g, attn_sinks, w_branch_gdn,
              w_branch_swa, w_out, norm2_g, w_ff_up, w_ff_down, final_norm_g):
    split_points = [int(s) for s in np.cumsum(PROJ_SIZES)[:-1]]
    for l in range(DEPTH):
        h = rms_norm(x, norm1_g[l])
        proj = h @ w_in[l]
        qkv_g, z, b_logit, a_logit, q_s, k_s, v_s, gate_logits = jnp.split(proj, split_points, axis=-1)
        y_gdn = gdn_branch(qkv_g, z, b_logit, a_logit, conv_w[l], a_log[l], dt_bias[l],
                           gdn_norm_g[l]) @ w_branch_gdn[l]
        y_swa = sliding_window_gqa(q_s, k_s, v_s, attn_sinks[l]) @ w_branch_swa[l]
        gates = jax.nn.sigmoid(gate_logits.astype(jnp.float32)).astype(x.dtype)
        g_gdn, g_swa = jnp.split(gates, N_BRANCH, axis=-1)
        x = x + (g_gdn * y_gdn + g_swa * y_swa) @ w_out[l]
        h = rms_norm(x, norm2_g[l])
        x = x + jnp.square(jax.nn.relu(h @ w_ff_up[l])) @ w_ff_down[l]
    return rms_norm(x, final_norm_g)


import jax as _jax
import jax.numpy as _jnp

TWIN_FORMAT = 'train_step'
FWD_PARAMS = ['x', 'norm1_g', 'w_in', 'conv_w', 'a_log', 'dt_bias', 'gdn_norm_g', 'attn_sinks', 'w_branch_gdn', 'w_branch_swa', 'w_out', 'norm2_g', 'w_ff_up', 'w_ff_down', 'final_norm_g']
TWIN_WEIGHTS = ['norm1_g', 'w_in', 'conv_w', 'a_log', 'dt_bias', 'gdn_norm_g', 'attn_sinks', 'w_branch_gdn', 'w_branch_swa', 'w_out', 'norm2_g', 'w_ff_up', 'w_ff_down', 'final_norm_g']
TWIN_DIFF_INPUT = 'x'
TWIN_INPUTS = ['x', 'norm1_g', 'w_in', 'conv_w', 'a_log', 'dt_bias', 'gdn_norm_g', 'attn_sinks', 'w_branch_gdn', 'w_branch_swa', 'w_out', 'norm2_g', 'w_ff_up', 'w_ff_down', 'final_norm_g', 'loss_target', 'm_norm1_g', 'm_w_in', 'm_conv_w', 'm_a_log', 'm_dt_bias', 'm_gdn_norm_g', 'm_attn_sinks', 'm_w_branch_gdn', 'm_w_branch_swa', 'm_w_out', 'm_norm2_g', 'm_w_ff_up', 'm_w_ff_down', 'm_final_norm_g', 'v_norm1_g', 'v_w_in', 'v_conv_w', 'v_a_log', 'v_dt_bias', 'v_gdn_norm_g', 'v_attn_sinks', 'v_w_branch_gdn', 'v_w_branch_swa', 'v_w_out', 'v_norm2_g', 'v_w_ff_up', 'v_w_ff_down', 'v_final_norm_g']
TWIN_OUTPUTS = ['loss', 'grad_x', 'grad_norm1_g', 'grad_w_in', 'grad_conv_w', 'grad_a_log', 'grad_dt_bias', 'grad_gdn_norm_g', 'grad_attn_sinks', 'grad_w_branch_gdn', 'grad_w_branch_swa', 'grad_w_out', 'grad_norm2_g', 'grad_w_ff_up', 'grad_w_ff_down', 'grad_final_norm_g', 'delta_norm1_g', 'delta_w_in', 'delta_conv_w', 'delta_a_log', 'delta_dt_bias', 'delta_gdn_norm_g', 'delta_attn_sinks', 'delta_w_branch_gdn', 'delta_w_branch_swa', 'delta_w_out', 'delta_norm2_g', 'delta_w_ff_up', 'delta_w_ff_down', 'delta_final_norm_g', 'new_m_norm1_g', 'new_m_w_in', 'new_m_conv_w', 'new_m_a_log', 'new_m_dt_bias', 'new_m_gdn_norm_g', 'new_m_attn_sinks', 'new_m_w_branch_gdn', 'new_m_w_branch_swa', 'new_m_w_out', 'new_m_norm2_g', 'new_m_w_ff_up', 'new_m_w_ff_down', 'new_m_final_norm_g', 'new_v_norm1_g', 'new_v_w_in', 'new_v_conv_w', 'new_v_a_log', 'new_v_dt_bias', 'new_v_gdn_norm_g', 'new_v_attn_sinks', 'new_v_w_branch_gdn', 'new_v_w_branch_swa', 'new_v_w_out', 'new_v_norm2_g', 'new_v_w_ff_up', 'new_v_w_ff_down', 'new_v_final_norm_g']
TWIN_LEAF_KINDS = {'loss': 'loss', 'grad_x': 'grad_x', 'grad_norm1_g': 'grad_w', 'grad_w_in': 'grad_w', 'grad_conv_w': 'grad_w', 'grad_a_log': 'grad_w', 'grad_dt_bias': 'grad_w', 'grad_gdn_norm_g': 'grad_w', 'grad_attn_sinks': 'grad_w', 'grad_w_branch_gdn': 'grad_w', 'grad_w_branch_swa': 'grad_w', 'grad_w_out': 'grad_w', 'grad_norm2_g': 'grad_w', 'grad_w_ff_up': 'grad_w', 'grad_w_ff_down': 'grad_w', 'grad_final_norm_g': 'grad_w', 'delta_norm1_g': 'delta_w', 'delta_w_in': 'delta_w', 'delta_conv_w': 'delta_w', 'delta_a_log': 'delta_w', 'delta_dt_bias': 'delta_w', 'delta_gdn_norm_g': 'delta_w', 'delta_attn_sinks': 'delta_w', 'delta_w_branch_gdn': 'delta_w', 'delta_w_branch_swa': 'delta_w', 'delta_w_out': 'delta_w', 'delta_norm2_g': 'delta_w', 'delta_w_ff_up': 'delta_w', 'delta_w_ff_down': 'delta_w', 'delta_final_norm_g': 'delta_w', 'new_m_norm1_g': 'new_m', 'new_m_w_in': 'new_m', 'new_m_conv_w': 'new_m', 'new_m_a_log': 'new_m', 'new_m_dt_bias': 'new_m', 'new_m_gdn_norm_g': 'new_m', 'new_m_attn_sinks': 'new_m', 'new_m_w_branch_gdn': 'new_m', 'new_m_w_branch_swa': 'new_m', 'new_m_w_out': 'new_m', 'new_m_norm2_g': 'new_m', 'new_m_w_ff_up': 'new_m', 'new_m_w_ff_down': 'new_m', 'new_m_final_norm_g': 'new_m', 'new_v_norm1_g': 'new_v', 'new_v_w_in': 'new_v', 'new_v_conv_w': 'new_v', 'new_v_a_log': 'new_v', 'new_v_dt_bias': 'new_v', 'new_v_gdn_norm_g': 'new_v', 'new_v_attn_sinks': 'new_v', 'new_v_w_branch_gdn': 'new_v', 'new_v_w_branch_swa': 'new_v', 'new_v_w_out': 'new_v', 'new_v_norm2_g': 'new_v', 'new_v_w_ff_up': 'new_v', 'new_v_w_ff_down': 'new_v', 'new_v_final_norm_g': 'new_v'}


def _forward(args):
    return _fwd_reference(*[args[k] for k in FWD_PARAMS])


def _output_shape():
    def fwd():
        inp = _fwd_setup_inputs(0)
        return _fwd_reference(*[inp[k] for k in FWD_PARAMS])
    out = _jax.eval_shape(fwd)
    return out.shape, out.dtype

N_MICROBATCH = 1
ADAM_LR = 0.001
ADAM_B1 = 0.9
ADAM_B2 = 0.999
ADAM_EPS = 1e-08
ADAM_WD = 0.01
ADAM_STEP = 10
PER_EXAMPLE_BATCH_AXIS = {'x': 0, 'loss_target': 0}
SHARED_INPUTS = []
_WEIGHT_DTYPES = {'norm1_g': _jnp.float32, 'w_in': _jnp.float32, 'conv_w': _jnp.float32, 'a_log': _jnp.float32, 'dt_bias': _jnp.float32, 'gdn_norm_g': _jnp.float32, 'attn_sinks': _jnp.float32, 'w_branch_gdn': _jnp.float32, 'w_branch_swa': _jnp.float32, 'w_out': _jnp.float32, 'norm2_g': _jnp.float32, 'w_ff_up': _jnp.float32, 'w_ff_down': _jnp.float32, 'final_norm_g': _jnp.float32}
MOMENT_SCALE = {'norm1_g': 4.596689e-02, 'w_in': 1.701085e-02, 'conv_w': 1.833047e-02, 'a_log': 7.700169e-02, 'dt_bias': 7.393623e-02, 'gdn_norm_g': 8.978130e-02, 'attn_sinks': 2.080229e-02, 'w_branch_gdn': 2.356255e-02, 'w_branch_swa': 1.855312e-02, 'w_out': 2.949122e-02, 'norm2_g': 7.209993e-02, 'w_ff_up': 3.589309e-02, 'w_ff_down': 7.716027e-02, 'final_norm_g': 1.655210e+01}


def _to_microbatches(a, axis):
    t = _jnp.moveaxis(a, axis, 0)
    t = t.reshape((N_MICROBATCH, t.shape[0] // N_MICROBATCH) + t.shape[1:])
    return _jnp.moveaxis(t, 1, axis + 1)


def setup_inputs(seed: int = 0) -> dict:
    inp = _fwd_setup_inputs(seed)
    key = _jax.random.fold_in(_jax.random.key(seed), 7919)
    shape, _ = _output_shape()
    out = dict(inp)
    out["loss_target"] = _jax.random.normal(_jax.random.fold_in(key, 0), shape, _jnp.float32)
    for i, name in enumerate(TWIN_WEIGHTS):
        w = inp[name].astype(_jnp.float32)
        if MOMENT_SCALE is None:
            s = _jnp.sqrt(_jnp.mean(_jnp.square(w)) + 1e-30)
        else:
            s = MOMENT_SCALE[name]
        km, kv = _jax.random.split(_jax.random.fold_in(key, i + 1))
        out[name] = w
        out["m_" + name] = s * _jax.random.normal(km, w.shape, _jnp.float32)
        out["v_" + name] = (s * s) * _jax.random.uniform(kv, w.shape, _jnp.float32, 0.5, 1.5)
    if N_MICROBATCH > 1:
        for name, axis in PER_EXAMPLE_BATCH_AXIS.items():
            out[name] = _to_microbatches(out[name], axis)
    return {'x': out['x'], 'norm1_g': out['norm1_g'], 'w_in': out['w_in'], 'conv_w': out['conv_w'], 'a_log': out['a_log'], 'dt_bias': out['dt_bias'], 'gdn_norm_g': out['gdn_norm_g'], 'attn_sinks': out['attn_sinks'], 'w_branch_gdn': out['w_branch_gdn'], 'w_branch_swa': out['w_branch_swa'], 'w_out': out['w_out'], 'norm2_g': out['norm2_g'], 'w_ff_up': out['w_ff_up'], 'w_ff_down': out['w_ff_down'], 'final_norm_g': out['final_norm_g'], 'loss_target': out['loss_target'], 'm_norm1_g': out['m_norm1_g'], 'm_w_in': out['m_w_in'], 'm_conv_w': out['m_conv_w'], 'm_a_log': out['m_a_log'], 'm_dt_bias': out['m_dt_bias'], 'm_gdn_norm_g': out['m_gdn_norm_g'], 'm_attn_sinks': out['m_attn_sinks'], 'm_w_branch_gdn': out['m_w_branch_gdn'], 'm_w_branch_swa': out['m_w_branch_swa'], 'm_w_out': out['m_w_out'], 'm_norm2_g': out['m_norm2_g'], 'm_w_ff_up': out['m_w_ff_up'], 'm_w_ff_down': out['m_w_ff_down'], 'm_final_norm_g': out['m_final_norm_g'], 'v_norm1_g': out['v_norm1_g'], 'v_w_in': out['v_w_in'], 'v_conv_w': out['v_conv_w'], 'v_a_log': out['v_a_log'], 'v_dt_bias': out['v_dt_bias'], 'v_gdn_norm_g': out['v_gdn_norm_g'], 'v_attn_sinks': out['v_attn_sinks'], 'v_w_branch_gdn': out['v_w_branch_gdn'], 'v_w_branch_swa': out['v_w_branch_swa'], 'v_w_out': out['v_w_out'], 'v_norm2_g': out['v_norm2_g'], 'v_w_ff_up': out['v_w_ff_up'], 'v_w_ff_down': out['v_w_ff_down'], 'v_final_norm_g': out['v_final_norm_g']}


def _loss(weights, diff, rest, loss_target):
    with _jax.named_scope("forward"):
        args = {**rest, TWIN_DIFF_INPUT: diff, **{k: w.astype(_WEIGHT_DTYPES[k]) for k, w in weights.items()}}
        y = _forward(args)
    with _jax.named_scope("loss_head"):
        err = _jnp.square(y.astype(_jnp.float32) - loss_target)
        return 0.5 * _jnp.sum(_jnp.mean(err, axis=-1)) if err.ndim else 0.5 * err


def _adamw(w, g, m, v):
    m = ADAM_B1 * m + (1.0 - ADAM_B1) * g
    v = ADAM_B2 * v + (1.0 - ADAM_B2) * _jnp.square(g)
    m_hat = m / (1.0 - ADAM_B1 ** ADAM_STEP)
    v_hat = v / (1.0 - ADAM_B2 ** ADAM_STEP)
    delta = -ADAM_LR * (m_hat / (_jnp.sqrt(v_hat) + ADAM_EPS) + ADAM_WD * w)
    return delta, m, v


def reference(x, norm1_g, w_in, conv_w, a_log, dt_bias, gdn_norm_g, attn_sinks, w_branch_gdn, w_branch_swa, w_out, norm2_g, w_ff_up, w_ff_down, final_norm_g, loss_target, m_norm1_g, m_w_in, m_conv_w, m_a_log, m_dt_bias, m_gdn_norm_g, m_attn_sinks, m_w_branch_gdn, m_w_branch_swa, m_w_out, m_norm2_g, m_w_ff_up, m_w_ff_down, m_final_norm_g, v_norm1_g, v_w_in, v_conv_w, v_a_log, v_dt_bias, v_gdn_norm_g, v_attn_sinks, v_w_branch_gdn, v_w_branch_swa, v_w_out, v_norm2_g, v_w_ff_up, v_w_ff_down, v_final_norm_g):
    given = dict(x=x, norm1_g=norm1_g, w_in=w_in, conv_w=conv_w, a_log=a_log, dt_bias=dt_bias, gdn_norm_g=gdn_norm_g, attn_sinks=attn_sinks, w_branch_gdn=w_branch_gdn, w_branch_swa=w_branch_swa, w_out=w_out, norm2_g=norm2_g, w_ff_up=w_ff_up, w_ff_down=w_ff_down, final_norm_g=final_norm_g, loss_target=loss_target, m_norm1_g=m_norm1_g, m_w_in=m_w_in, m_conv_w=m_conv_w, m_a_log=m_a_log, m_dt_bias=m_dt_bias, m_gdn_norm_g=m_gdn_norm_g, m_attn_sinks=m_attn_sinks, m_w_branch_gdn=m_w_branch_gdn, m_w_branch_swa=m_w_branch_swa, m_w_out=m_w_out, m_norm2_g=m_norm2_g, m_w_ff_up=m_w_ff_up, m_w_ff_down=m_w_ff_down, m_final_norm_g=m_final_norm_g, v_norm1_g=v_norm1_g, v_w_in=v_w_in, v_conv_w=v_conv_w, v_a_log=v_a_log, v_dt_bias=v_dt_bias, v_gdn_norm_g=v_gdn_norm_g, v_attn_sinks=v_attn_sinks, v_w_branch_gdn=v_w_branch_gdn, v_w_branch_swa=v_w_branch_swa, v_w_out=v_w_out, v_norm2_g=v_norm2_g, v_w_ff_up=v_w_ff_up, v_w_ff_down=v_w_ff_down, v_final_norm_g=v_final_norm_g)
    weights = {n: given[n] for n in TWIN_WEIGHTS}
    shared = {n: given[n] for n in SHARED_INPUTS}
    per_example = {n: given[n] for n in ['x']}
    grad_fn = _jax.value_and_grad(_loss, argnums=(0, 1))

    def one_microbatch(ex, loss_target):
        ex = dict(ex)
        diff = ex.pop(TWIN_DIFF_INPUT)
        return grad_fn(weights, diff, {**shared, **ex}, loss_target)

    if N_MICROBATCH == 1:
        loss, (grad_w, grad_x) = one_microbatch(per_example, given["loss_target"])
    else:
        def body(carry, xs):
            loss_sum, grad_sum = carry
            l_k, (gw_k, gx_k) = one_microbatch(xs[0], xs[1])
            with _jax.named_scope("update"):
                return (loss_sum + l_k, _jax.tree.map(_jnp.add, grad_sum, gw_k)), gx_k

        init = (_jnp.zeros((), _jnp.float32), _jax.tree.map(_jnp.zeros_like, weights))
        (loss, grad_w), grad_x = _jax.lax.scan(body, init, (per_example, given["loss_target"]))
    with _jax.named_scope("update"):
        delta_w, new_m, new_v = {}, {}, {}
        for n in TWIN_WEIGHTS:
            delta_w[n], new_m[n], new_v[n] = _adamw(weights[n], grad_w[n], given["m_" + n], given["v_" + n])
    return (loss, grad_x, *[grad_w[n] for n in TWIN_WEIGHTS], *[delta_w[n] for n in TWIN_WEIGHTS],
            *[new_m[n] for n in TWIN_WEIGHTS], *[new_v[n] for n in TWIN_WEIGHTS])
```

```python
import functools
import math

import jax
import jax.numpy as jnp
from jax import lax
from jax.experimental import pallas as pl
from jax.experimental.pallas import tpu as pltpu

F32 = jnp.float32
BF16 = jnp.bfloat16
HI = lax.Precision.HIGHEST
MESH = pl.DeviceIdType.MESH

N_DEV = 8
D_MODEL = 2048
DEPTH = 4
GDN_HEADS = 16
HEAD_DIM = 128
CHUNK = 64
CONV_K = 4
SWA_HEADS = 32
SWA_KV = 4
SWA_GROUP = SWA_HEADS // SWA_KV
SWA_DIM = 64
WINDOW = 128
D_FF = 4 * D_MODEL
D_IN = 14880
NORM_EPS = 1e-6
GDN_W = GDN_HEADS * HEAD_DIM
KV_W = SWA_KV * SWA_DIM

QKV0, Z0, GG0, GS0, QS0, KS0, VS0, BA0 = 0, 6144, 8192, 10240, 12288, 14336, 14592, 14848
N_PROJ = 15360
_SEGS = ((0, 6144, QKV0), (6144, 2048, Z0), (8192, 32, BA0), (8224, 2048, QS0), (10272, 256, KS0),
         (10528, 256, VS0), (10784, 4096, GG0))

ADAM_LR, ADAM_B1, ADAM_B2, ADAM_EPS, ADAM_WD, ADAM_STEP = 0.001, 0.9, 0.999, 1e-08, 0.01, 10

VMEM_LIMIT = 56 * 1024 * 1024
GDN_HB = 4


def _cp(*sem):
    return pltpu.CompilerParams(dimension_semantics=sem, vmem_limit_bytes=VMEM_LIMIT)


def _tile(n, want):
    t = min(n, want)
    while n % t:
        t -= 128
    return t


def _matmul(a, b, mode, out_shapes, epilogue=None, extras=(), name="mm"):
    if mode == "tn":
        K, M = a.shape
    else:
        M, K = a.shape
    N = b.shape[0] if mode == "nt" else b.shape[1]
    tm, tn, tk = _tile(M, 1024), _tile(N, 1024), _tile(K, 512)
    nk = K // tk
    dims = {"nn": (((1,), (0,)), ((), ())), "nt": (((1,), (1,)), ((), ())), "tn": (((0,), (0,)), ((), ()))}[mode]
    n_ex, n_out = len(extras), len(out_shapes)

    def body(*refs):
        a_ref, b_ref = refs[:2]
        ex = refs[2:2 + n_ex]
        outs = refs[2 + n_ex:2 + n_ex + n_out]
        acc = refs[-1]
        k = pl.program_id(2)

        @pl.when(k == 0)
        def _():
            acc[...] = jnp.zeros_like(acc)

        acc[...] += lax.dot_general(a_ref[...].astype(BF16), b_ref[...].astype(BF16), dims,
                                    preferred_element_type=F32)

        @pl.when(k == nk - 1)
        def _():
            res = (acc[...],) if epilogue is None else epilogue(acc[...], *[e[...] for e in ex])
            for o, r in zip(outs, res):
                o[...] = r.astype(o.dtype)

    a_spec = pl.BlockSpec((tk, tm), lambda i, j, k: (k, i)) if mode == "tn" else pl.BlockSpec((tm, tk), lambda i, j, k: (i, k))
    b_spec = pl.BlockSpec((tn, tk), lambda i, j, k: (j, k)) if mode == "nt" else pl.BlockSpec((tk, tn), lambda i, j, k: (k, j))
    mn = pl.BlockSpec((tm, tn), lambda i, j, k: (i, j))
    return pl.pallas_call(
        body, grid=(M // tm, N // tn, nk),
        in_specs=[a_spec, b_spec] + [mn] * n_ex, out_specs=[mn] * n_out,
        out_shape=[jax.ShapeDtypeStruct((M, N), dt) for dt in out_shapes],
        scratch_shapes=[pltpu.VMEM((tm, tn), F32)],
        compiler_params=_cp("parallel", "parallel", "arbitrary"), name=name,
    )(a, b, *extras)


def _rms_fwd(x, g, name):
    T, D = x.shape
    tr = _tile(T, 256)

    def body(x_ref, g_ref, h_ref):
        xv = x_ref[...]
        r = lax.rsqrt(jnp.mean(xv * xv, axis=-1, keepdims=True) + NORM_EPS)
        h_ref[...] = (xv * r * g_ref[...]).astype(BF16)

    return pl.pallas_call(
        body, grid=(T // tr,),
        in_specs=[pl.BlockSpec((tr, D), lambda i: (i, 0)), pl.BlockSpec((1, D), lambda i: (0, 0))],
        out_specs=pl.BlockSpec((tr, D), lambda i: (i, 0)),
        out_shape=jax.ShapeDtypeStruct((T, D), BF16), compiler_params=_cp("parallel"), name=name,
    )(x, g.reshape(1, D))


def _rms_bwd(dh, x, g, dres, name):
    T, D = x.shape
    tr = _tile(T, 256)

    def body(dh_ref, x_ref, g_ref, dres_ref, dx_ref, dg_ref):
        @pl.when(pl.program_id(0) == 0)
        def _():
            dg_ref[...] = jnp.zeros_like(dg_ref)

        xv = x_ref[...]
        r = lax.rsqrt(jnp.mean(xv * xv, axis=-1, keepdims=True) + NORM_EPS)
        xhat = xv * r
        dhv = dh_ref[...].astype(F32)
        gd = dhv * g_ref[...]
        dx_ref[...] = dres_ref[...] + r * (gd - xhat * jnp.mean(gd * xhat, axis=-1, keepdims=True))
        dg_ref[...] += jnp.sum(dhv * xhat, axis=0, keepdims=True)

    row = pl.BlockSpec((tr, D), lambda i: (i, 0))
    vec = pl.BlockSpec((1, D), lambda i: (0, 0))
    return pl.pallas_call(
        body, grid=(T // tr,), in_specs=[row, row, vec, row], out_specs=[row, vec],
        out_shape=[jax.ShapeDtypeStruct((T, D), F32), jax.ShapeDtypeStruct((1, D), F32)],
        compiler_params=_cp("arbitrary"), name=name,
    )(dh, x, g.reshape(1, D), dres)


def _loss_head(x, g, tgt):
    T, D = x.shape
    tr = _tile(T, 256)

    def body(x_ref, g_ref, t_ref, loss_ref, dx_ref, dg_ref):
        @pl.when(pl.program_id(0) == 0)
        def _():
            dg_ref[...] = jnp.zeros_like(dg_ref)
            loss_ref[...] = jnp.zeros_like(loss_ref)

        xv = x_ref[...]
        r = lax.rsqrt(jnp.mean(xv * xv, axis=-1, keepdims=True) + NORM_EPS)
        xhat = xv * r
        err = xhat * g_ref[...] - t_ref[...]
        loss_ref[...] += (0.5 / D) * jnp.sum(jnp.sum(err * err, axis=-1, keepdims=True), axis=0, keepdims=True)
        dy = err * (1.0 / D)
        gd = dy * g_ref[...]
        dx_ref[...] = r * (gd - xhat * jnp.mean(gd * xhat, axis=-1, keepdims=True))
        dg_ref[...] += jnp.sum(dy * xhat, axis=0, keepdims=True)

    row = pl.BlockSpec((tr, D), lambda i: (i, 0))
    vec = pl.BlockSpec((1, D), lambda i: (0, 0))
    return pl.pallas_call(
        body, grid=(T // tr,), in_specs=[row, vec, row],
        out_specs=[pl.BlockSpec((1, 128), lambda i: (0, 0)), row, vec],
        out_shape=[jax.ShapeDtypeStruct((1, 128), F32), jax.ShapeDtypeStruct((T, D), F32),
                   jax.ShapeDtypeStruct((1, D), F32)],
        compiler_params=_cp("arbitrary"), name="loss_head",
    )(x, g.reshape(1, D), tgt)


def _sigmoid(x):
    return 1.0 / (1.0 + jnp.exp(-x))


def _gate_mix_fwd(proj, y_gdn, y_swa):
    T = proj.shape[0]
    tr, tc = _tile(T, 512), 512
    nc = D_MODEL // tc

    def body(gg_ref, gs_ref, yg_ref, ys_ref, mix_ref):
        mix_ref[...] = (_sigmoid(gg_ref[...]) * yg_ref[...] + _sigmoid(gs_ref[...]) * ys_ref[...]).astype(BF16)

    blk = pl.BlockSpec((tr, tc), lambda i, j: (i, j))
    return pl.pallas_call(
        body, grid=(T // tr, nc),
        in_specs=[pl.BlockSpec((tr, tc), lambda i, j: (i, GG0 // tc + j)),
                  pl.BlockSpec((tr, tc), lambda i, j: (i, GS0 // tc + j)), blk, blk],
        out_specs=blk, out_shape=jax.ShapeDtypeStruct((T, D_MODEL), BF16),
        compiler_params=_cp("parallel", "parallel"), name="gate_mix_fwd",
    )(proj, proj, y_gdn, y_swa)


def _gate_mix_bwd(dmix, proj, y_gdn, y_swa):
    T = proj.shape[0]
    tr, tc = _tile(T, 512), 512
    nc = D_MODEL // tc

    def body(dm_ref, gg_ref, gs_ref, yg_ref, ys_ref, dyg_ref, dys_ref, dgg_ref, dgs_ref):
        dm = dm_ref[...]
        sg, ss = _sigmoid(gg_ref[...]), _sigmoid(gs_ref[...])
        dyg_ref[...] = (dm * sg).astype(BF16)
        dys_ref[...] = (dm * ss).astype(BF16)
        dgg_ref[...] = (dm * yg_ref[...] * sg * (1.0 - sg)).astype(BF16)
        dgs_ref[...] = (dm * ys_ref[...] * ss * (1.0 - ss)).astype(BF16)

    blk = pl.BlockSpec((tr, tc), lambda i, j: (i, j))
    out = jax.ShapeDtypeStruct((T, D_MODEL), BF16)
    return pl.pallas_call(
        body, grid=(T // tr, nc),
        in_specs=[blk, pl.BlockSpec((tr, tc), lambda i, j: (i, GG0 // tc + j)),
                  pl.BlockSpec((tr, tc), lambda i, j: (i, GS0 // tc + j)), blk, blk],
        out_specs=[blk] * 4, out_shape=[out] * 4,
        compiler_params=_cp("parallel", "parallel"), name="gate_mix_bwd",
    )(dmix, proj, proj, y_gdn, y_swa)


CONV_TT, CONV_TW, HALO = 512, 256, 8


def _conv_fwd(proj, conv_w):
    T = proj.shape[0]
    tt = _tile(T, CONV_TT)
    W = 3 * GDN_W

    def body(x_ref, halo_ref, w_ref, y_ref):
        first = pl.program_id(1) == 0
        halo = jnp.where(first, 0.0, halo_ref[...])
        xe = jnp.concatenate([halo, x_ref[...]], axis=0)
        acc = xe[HALO:] * w_ref[pl.ds(CONV_K - 1, 1), :]
        for j in range(CONV_K - 1):
            acc = acc + pltpu.roll(xe, CONV_K - 1 - j, 0)[HALO:] * w_ref[pl.ds(j, 1), :]
        y_ref[...] = acc

    return pl.pallas_call(
        body, grid=(W // CONV_TW, T // tt),
        in_specs=[pl.BlockSpec((tt, CONV_TW), lambda c, t: (t, c)),
                  pl.BlockSpec((HALO, CONV_TW), lambda c, t: (jnp.maximum(t * (tt // HALO) - 1, 0), c)),
                  pl.BlockSpec((CONV_K, CONV_TW), lambda c, t: (0, c))],
        out_specs=pl.BlockSpec((tt, CONV_TW), lambda c, t: (t, c)),
        out_shape=jax.ShapeDtypeStruct((T, W), F32),
        compiler_params=_cp("parallel", "parallel"), name="conv_fwd",
    )(proj, proj, conv_w)


def _conv_bwd(dy, proj, conv_w):
    T = proj.shape[0]
    tt = _tile(T, CONV_TT)
    nt = T // tt
    W = 3 * GDN_W

    def body(dy_ref, dnext_ref, x_ref, halo_ref, w_ref, dx_ref, dw_ref):
        t = pl.program_id(1)

        @pl.when(t == 0)
        def _():
            dw_ref[...] = jnp.zeros_like(dw_ref)

        dyv = dy_ref[...]
        dye = jnp.concatenate([dyv, jnp.where(t == nt - 1, 0.0, dnext_ref[...])], axis=0)
        xe = jnp.concatenate([jnp.where(t == 0, 0.0, halo_ref[...]), x_ref[...]], axis=0)
        acc = dyv * w_ref[pl.ds(CONV_K - 1, 1), :]
        dw_ref[pl.ds(CONV_K - 1, 1), :] += jnp.sum(dyv * xe[HALO:], axis=0, keepdims=True)
        for s in range(1, CONV_K):
            j = CONV_K - 1 - s
            acc = acc + pltpu.roll(dye, tt + HALO - s, 0)[:tt] * w_ref[pl.ds(j, 1), :]
            dw_ref[pl.ds(j, 1), :] += jnp.sum(dyv * pltpu.roll(xe, s, 0)[HALO:], axis=0, keepdims=True)
        dx_ref[...] = acc.astype(BF16)

    cur = pl.BlockSpec((tt, CONV_TW), lambda c, t: (t, c))
    return pl.pallas_call(
        body, grid=(W // CONV_TW, nt),
        in_specs=[cur,
                  pl.BlockSpec((HALO, CONV_TW), lambda c, t: (jnp.minimum((t + 1) * (tt // HALO), T // HALO - 1), c)),
                  cur,
                  pl.BlockSpec((HALO, CONV_TW), lambda c, t: (jnp.maximum(t * (tt // HALO) - 1, 0), c)),
                  pl.BlockSpec((CONV_K, CONV_TW), lambda c, t: (0, c))],
        out_specs=[cur, pl.BlockSpec((CONV_K, CONV_TW), lambda c, t: (0, c))],
        out_shape=[jax.ShapeDtypeStruct((T, W), BF16), jax.ShapeDtypeStruct((CONV_K, W), F32)],
        compiler_params=_cp("parallel", "arbitrary"), name="conv_bwd",
    )(dy, dy, proj, proj, conv_w)


def _bmm(a, b, dims):
    return lax.dot_general(a, b, dims, precision=HI, preferred_element_type=F32)


def _bmm_nn(a, b):
    return _bmm(a, b, (((2,), (1,)), ((0,), (0,))))


def _bmm_nt(a, b):
    return _bmm(a, b, (((2,), (2,)), ((0,), (0,))))


def _bmm_tn(a, b):
    return _bmm(a, b, (((1,), (1,)), ((0,), (0,))))


def _col_from_row(row, eye):
    return jnp.sum(jnp.where(eye, row, 0.0), axis=2, keepdims=True)


def _silu(x):
    return x / (1.0 + jnp.exp(-x))


def _gdn_chunk(yq, yk, yv, z, a_row, b_row, a_log, dt_bias, norm_g, state):
    h = yq.shape[0]
    ii = lax.broadcasted_iota(jnp.int32, (1, CHUNK, CHUNK), 1)
    jj = lax.broadcasted_iota(jnp.int32, (1, CHUNK, CHUNK), 2)
    eye = ii == jj
    qr, kr, v = _silu(yq), _silu(yk), _silu(yv)
    q = qr * lax.rsqrt(jnp.sum(qr * qr, axis=-1, keepdims=True) + NORM_EPS) * (HEAD_DIM ** -0.5)
    k = kr * lax.rsqrt(jnp.sum(kr * kr, axis=-1, keepdims=True) + NORM_EPS)
    beta_row = _sigmoid(b_row)
    xa = a_row + dt_bias
    g_row = -jnp.exp(a_log) * (jnp.maximum(xa, 0.0) + jnp.log(1.0 + jnp.exp(-jnp.abs(xa))))
    upper = jnp.broadcast_to(jnp.where(ii <= jj, 1.0, 0.0).astype(F32), (h, CHUNK, CHUNK))
    decay_row = _bmm_nn(g_row, upper)
    decay_col = _col_from_row(decay_row, eye)
    beta_col = _col_from_row(beta_row, eye)
    decay_last = jnp.sum(g_row, axis=2, keepdims=True)
    gamma = jnp.exp(jnp.where(ii >= jj, decay_col - decay_row, -jnp.inf))
    k_beta = k * beta_col
    a_low = jnp.where(ii > jj, _bmm_nt(k_beta, k) * gamma, 0.0)
    t_inv = jnp.where(eye, 1.0, 0.0).astype(F32) - a_low
    p = a_low
    for _ in range(5):
        p = _bmm_nn(p, p)
        t_inv = t_inv + _bmm_nn(t_inv, p)
    e_dec = jnp.exp(decay_col)
    u = _bmm_nn(t_inv, v * beta_col)
    w = _bmm_nn(t_inv, k_beta * e_dec)
    qk = _bmm_nt(q, k) * gamma
    v_new = u - _bmm_nn(w, state)
    o = _bmm_nn(q * e_dec, state) + _bmm_nn(qk, v_new)
    s_new = state * jnp.exp(decay_last) + _bmm_tn(k * jnp.exp(decay_last - decay_col), v_new)
    o_n = o * lax.rsqrt(jnp.mean(o * o, axis=-1, keepdims=True) + NORM_EPS) * norm_g
    return o_n * _silu(z), s_new


def _heads(ref, hb):
    return jnp.stack([ref[:, HEAD_DIM * i:HEAD_DIM * (i + 1)] for i in range(hb)], axis=0)


def _gdn_fwd(y, proj, a_rows, b_rows, a_log, dt_bias, norm_g):
    T = y.shape[0]
    H, HB = GDN_HEADS, GDN_HB
    NC, HG, BW = T // CHUNK, GDN_HEADS // GDN_HB, GDN_HB * HEAD_DIM

    def body(q_ref, k_ref, v_ref, z_ref, a_ref, b_ref, alog_ref, dt_ref, ng_ref, o_ref, sst_ref, s_scr):
        @pl.when(pl.program_id(1) == 0)
        def _():
            s_scr[...] = jnp.zeros_like(s_scr)

        state = s_scr[...]
        sst_ref[...] = state
        o_g, s_new = _gdn_chunk(_heads(q_ref, HB), _heads(k_ref, HB), _heads(v_ref, HB), _heads(z_ref, HB),
                                a_ref[...], b_ref[...], alog_ref[...], dt_ref[...], ng_ref[...], state)
        s_scr[...] = s_new
        for i in range(HB):
            o_ref[:, HEAD_DIM * i:HEAD_DIM * (i + 1)] = o_g[i].astype(BF16)

    col = lambda off: pl.BlockSpec((CHUNK, BW), lambda hg, n, off=off: (n, off + hg))
    row = pl.BlockSpec((HB, None, 1, CHUNK), lambda hg, n: (hg, n, 0, 0))
    sc = pl.BlockSpec((HB, 1, 1), lambda hg, n: (hg, 0, 0))
    return pl.pallas_call(
        body, grid=(HG, NC),
        in_specs=[col(0), col(HG), col(2 * HG), col(Z0 // BW), row, row, sc, sc,
                  pl.BlockSpec((1, 1, HEAD_DIM), lambda hg, n: (0, 0, 0))],
        out_specs=[col(0), pl.BlockSpec((HB, None, HEAD_DIM, HEAD_DIM), lambda hg, n: (hg, n, 0, 0))],
        out_shape=[jax.ShapeDtypeStruct((T, GDN_W), BF16), jax.ShapeDtypeStruct((H, NC, HEAD_DIM, HEAD_DIM), F32)],
        scratch_shapes=[pltpu.VMEM((HB, HEAD_DIM, HEAD_DIM), F32)],
        compiler_params=_cp("parallel", "arbitrary"), name="gdn_fwd",
    )(y, y, y, proj, a_rows, b_rows, a_log, dt_bias, norm_g)


def _gdn_bwd(y, proj, a_rows, b_rows, a_log, dt_bias, norm_g, states, do):
    T = y.shape[0]
    H, HB = GDN_HEADS, GDN_HB
    NC, HG, BW = T // CHUNK, GDN_HEADS // GDN_HB, GDN_HB * HEAD_DIM

    def body(q_ref, k_ref, v_ref, z_ref, a_ref, b_ref, alog_ref, dt_ref, ng_ref, sst_ref, do_ref,
             dq_ref, dk_ref, dv_ref, dz_ref, da_ref, db_ref, dalog_ref, ddt_ref, dng_ref, ds_scr):
        hg, n = pl.program_id(0), pl.program_id(1)

        @pl.when(n == 0)
        def _():
            ds_scr[...] = jnp.zeros_like(ds_scr)
            dalog_ref[...] = jnp.zeros_like(dalog_ref)
            ddt_ref[...] = jnp.zeros_like(ddt_ref)

        @pl.when((n == 0) & (hg == 0))
        def _():
            dng_ref[...] = jnp.zeros_like(dng_ref)

        args = (_heads(q_ref, HB), _heads(k_ref, HB), _heads(v_ref, HB), _heads(z_ref, HB), a_ref[...], b_ref[...],
                alog_ref[...], dt_ref[...], ng_ref[...], sst_ref[...])
        _, vjp = jax.vjp(_gdn_chunk, *args)
        dq, dk, dv, dz, da, db, dalog, ddt, dng, d_state = vjp((_heads(do_ref, HB).astype(F32), ds_scr[...]))
        ds_scr[...] = d_state
        for i in range(HB):
            sl = slice(HEAD_DIM * i, HEAD_DIM * (i + 1))
            dq_ref[:, sl] = dq[i]
            dk_ref[:, sl] = dk[i]
            dv_ref[:, sl] = dv[i]
            dz_ref[:, sl] = dz[i].astype(BF16)
        da_ref[...] = da
        db_ref[...] = db
        dalog_ref[...] += dalog
        ddt_ref[...] += ddt
        dng_ref[...] += dng

    rev = lambda n: NC - 1 - n
    col = lambda off: pl.BlockSpec((CHUNK, BW), lambda hg, n, off=off: (rev(n), off + hg))
    row = pl.BlockSpec((HB, None, 1, CHUNK), lambda hg, n: (hg, rev(n), 0, 0))
    sc = pl.BlockSpec((HB, 1, 1), lambda hg, n: (hg, 0, 0))
    ng = pl.BlockSpec((1, 1, HEAD_DIM), lambda hg, n: (0, 0, 0))
    sst = pl.BlockSpec((HB, None, HEAD_DIM, HEAD_DIM), lambda hg, n: (hg, rev(n), 0, 0))
    rw = jax.ShapeDtypeStruct((H, NC, 1, CHUNK), F32)
    s1 = jax.ShapeDtypeStruct((H, 1, 1), F32)
    dq, dk, dv, dz, da, db, dalog, ddt, dng = pl.pallas_call(
        body, grid=(HG, NC),
        in_specs=[col(0), col(HG), col(2 * HG), col(Z0 // BW), row, row, sc, sc, ng, sst, col(0)],
        out_specs=[col(0), col(0), col(0), col(0), row, row, sc, sc, ng],
        out_shape=[jax.ShapeDtypeStruct((T, GDN_W), F32)] * 3 + [jax.ShapeDtypeStruct((T, GDN_W), BF16), rw, rw, s1, s1,
                                                                jax.ShapeDtypeStruct((1, 1, HEAD_DIM), F32)],
        scratch_shapes=[pltpu.VMEM((HB, HEAD_DIM, HEAD_DIM), F32)],
        compiler_params=_cp("arbitrary", "arbitrary"), name="gdn_bwd",
    )(y, y, y, proj, a_rows, b_rows, a_log, dt_bias, norm_g, states, do)
    return jnp.concatenate([dq, dk, dv], axis=1), dz, da, db, dalog, ddt, dng


def _swa_block(q, k_prev, k_cur, v_prev, v_cur, sink, slope, has_prev):
    kb = jnp.concatenate([k_prev, k_cur], axis=0)
    vb = jnp.concatenate([v_prev, v_cur], axis=0)
    q2 = q.reshape(SWA_GROUP * WINDOW, SWA_DIM)
    s = lax.dot_general(q2, kb, (((1,), (1,)), ((), ())), precision=HI, preferred_element_type=F32)
    s = s.reshape(SWA_GROUP, WINDOW, 2 * WINDOW) * (SWA_DIM ** -0.5)
    qi = lax.broadcasted_iota(jnp.int32, (1, WINDOW, 2 * WINDOW), 1)
    sj = lax.broadcasted_iota(jnp.int32, (1, WINDOW, 2 * WINDOW), 2)
    dist = qi + WINDOW - sj
    valid = (dist >= 0) & (dist < WINDOW) & (has_prev | (sj >= WINDOW))
    s = jnp.where(valid, s - slope * dist.astype(F32), -jnp.inf)
    m = lax.stop_gradient(jnp.maximum(jnp.max(s, axis=-1, keepdims=True), sink))
    p = jnp.exp(s - m)
    probs = p / (jnp.sum(p, axis=-1, keepdims=True) + jnp.exp(sink - m))
    o = lax.dot_general(probs.reshape(SWA_GROUP * WINDOW, 2 * WINDOW), vb, (((1,), (0,)), ((), ())),
                        precision=HI, preferred_element_type=F32)
    return o.reshape(SWA_GROUP, WINDOW, SWA_DIM)


def _alibi_slopes():
    return (2.0 ** (-8.0 * jnp.arange(1, SWA_HEADS + 1, dtype=F32) / SWA_HEADS)).reshape(SWA_HEADS, 1, 1)


def _swa_fwd(q, k, v, sinks):
    T = q.shape[1]
    NB = T // WINDOW

    def body(q_ref, kp_ref, kc_ref, vp_ref, vc_ref, sink_ref, slope_ref, o_ref):
        o = _swa_block(q_ref[...], kp_ref[...], kc_ref[...], vp_ref[...], vc_ref[...], sink_ref[...],
                       slope_ref[...], pl.program_id(1) > 0)
        o_ref[...] = o.astype(BF16)

    qs = pl.BlockSpec((SWA_GROUP, WINDOW, SWA_DIM), lambda h, n: (h, n, 0))
    cur = pl.BlockSpec((None, WINDOW, SWA_DIM), lambda h, n: (h, n, 0))
    prev = pl.BlockSpec((None, WINDOW, SWA_DIM), lambda h, n: (h, jnp.maximum(n - 1, 0), 0))
    hs = pl.BlockSpec((SWA_GROUP, 1, 1), lambda h, n: (h, 0, 0))
    return pl.pallas_call(
        body, grid=(SWA_KV, NB), in_specs=[qs, prev, cur, prev, cur, hs, hs], out_specs=qs,
        out_shape=jax.ShapeDtypeStruct((SWA_HEADS, T, SWA_DIM), BF16),
        compiler_params=_cp("parallel", "parallel"), name="swa_fwd",
    )(q, k, k, v, v, sinks, _alibi_slopes())


def _swa_bwd(q, k, v, sinks, do):
    T = q.shape[1]
    NB = T // WINDOW

    def body(q_ref, kp_ref, kc_ref, vp_ref, vc_ref, sink_ref, slope_ref, do_ref,
             dq_ref, dk_ref, dv_ref, dsink_ref, dk_scr, dv_scr):
        n = pl.program_id(1)

        @pl.when(n == 0)
        def _():
            dk_scr[...] = jnp.zeros_like(dk_scr)
            dv_scr[...] = jnp.zeros_like(dv_scr)
            dsink_ref[...] = jnp.zeros_like(dsink_ref)

        has_prev = n < NB - 1
        fn = functools.partial(_swa_block, slope=slope_ref[...], has_prev=has_prev)
        _, vjp = jax.vjp(fn, q_ref[...], kp_ref[...], kc_ref[...], vp_ref[...], vc_ref[...], sink_ref[...])
        dq, dkp, dkc, dvp, dvc, dsink = vjp(do_ref[...].astype(F32))
        dq_ref[...] = dq.astype(BF16)
        dk_ref[...] = (dkc + dk_scr[...]).astype(BF16)
        dv_ref[...] = (dvc + dv_scr[...]).astype(BF16)
        dk_scr[...] = dkp
        dv_scr[...] = dvp
        dsink_ref[...] += dsink

    rev = lambda n: NB - 1 - n
    qs = pl.BlockSpec((SWA_GROUP, WINDOW, SWA_DIM), lambda h, n: (h, rev(n), 0))
    cur = pl.BlockSpec((None, WINDOW, SWA_DIM), lambda h, n: (h, rev(n), 0))
    prev = pl.BlockSpec((None, WINDOW, SWA_DIM), lambda h, n: (h, jnp.maximum(rev(n) - 1, 0), 0))
    hs = pl.BlockSpec((SWA_GROUP, 1, 1), lambda h, n: (h, 0, 0))
    kv = jax.ShapeDtypeStruct((SWA_KV, T, SWA_DIM), BF16)
    return pl.pallas_call(
        body, grid=(SWA_KV, NB), in_specs=[qs, prev, cur, prev, cur, hs, hs, qs],
        out_specs=[qs, cur, cur, hs],
        out_shape=[jax.ShapeDtypeStruct((SWA_HEADS, T, SWA_DIM), BF16), kv, kv,
                   jax.ShapeDtypeStruct((SWA_HEADS, 1, 1), F32)],
        scratch_shapes=[pltpu.VMEM((WINDOW, SWA_DIM), F32), pltpu.VMEM((WINDOW, SWA_DIM), F32)],
        compiler_params=_cp("parallel", "arbitrary"), name="swa_bwd",
    )(q, k, k, v, v, sinks, _alibi_slopes(), do)


def _to_heads(t, n_heads):
    return t.reshape(t.shape[0], n_heads, SWA_DIM).transpose(1, 0, 2)


def _from_heads(t):
    return t.transpose(1, 0, 2).reshape(t.shape[1], -1)


def _rows(t):
    return t.T.reshape(t.shape[1], t.shape[0] // CHUNK, 1, CHUNK)


def _unrows(t):
    return t.reshape(t.shape[0], -1).T


def _branch_inputs(proj):
    b_rows = _rows(proj[:, BA0:BA0 + GDN_HEADS])
    a_rows = _rows(proj[:, BA0 + GDN_HEADS:BA0 + 2 * GDN_HEADS])
    q_s = _to_heads(proj[:, QS0:QS0 + D_MODEL], SWA_HEADS)
    k_s = _to_heads(proj[:, KS0:KS0 + KV_W], SWA_KV)
    v_s = _to_heads(proj[:, VS0:VS0 + KV_W], SWA_KV)
    return a_rows, b_rows, q_s, k_s, v_s


def _relu2_epilogue(acc):
    r = jnp.maximum(acc, 0.0)
    return acc, r * r


def _layer_fwd(x, w):
    h1 = _rms_fwd(x, w["norm1_g"], "rms1_fwd")
    proj, = _matmul(h1, w["w_in"], "nn", [F32], name="mm_proj")
    a_rows, b_rows, q_s, k_s, v_s = _branch_inputs(proj)
    y = _conv_fwd(proj, w["conv_w"])
    o_gdn, states = _gdn_fwd(y, proj, a_rows, b_rows, w["a_log"].reshape(-1, 1, 1), w["dt_bias"].reshape(-1, 1, 1),
                             w["gdn_norm_g"].reshape(1, 1, -1))
    o_swa = _from_heads(_swa_fwd(q_s, k_s, v_s, w["attn_sinks"].reshape(-1, 1, 1)))
    y_gdn, = _matmul(o_gdn, w["w_branch_gdn"], "nn", [F32], name="mm_bgdn")
    y_swa, = _matmul(o_swa, w["w_branch_swa"], "nn", [F32], name="mm_bswa")
    mix = _gate_mix_fwd(proj, y_gdn, y_swa)
    x2, = _matmul(mix, w["w_out"], "nn", [F32], epilogue=lambda acc, r: (r + acc,), extras=(x,), name="mm_out")
    h2 = _rms_fwd(x2, w["norm2_g"], "rms2_fwd")
    u, act = _matmul(h2, w["w_ff_up"], "nn", [F32, BF16], epilogue=_relu2_epilogue, name="mm_up")
    x3, = _matmul(act, w["w_ff_down"], "nn", [F32], epilogue=lambda acc, r: (r + acc,), extras=(x2,), name="mm_down")
    saved = dict(x=x, h1=h1, proj=proj, y=y, states=states, o_gdn=o_gdn, o_swa=o_swa, y_gdn=y_gdn, y_swa=y_swa,
                 mix=mix, x2=x2, h2=h2, u=u, act=act)
    return x3, saved


def _layer_bwd(dx3, w, s):
    T = dx3.shape[0]
    g = {}
    dx3b = dx3.astype(BF16)
    du, = _matmul(dx3b, w["w_ff_down"], "nt", [BF16], extras=(s["u"],),
                  epilogue=lambda acc, uu: (acc * 2.0 * jnp.maximum(uu, 0.0),), name="mm_down_dx")
    g["w_ff_down"], = _matmul(s["act"], dx3b, "tn", [F32], name="mm_down_dw")
    dh2, = _matmul(du, w["w_ff_up"], "nt", [F32], name="mm_up_dx")
    g["w_ff_up"], = _matmul(s["h2"], du, "tn", [F32], name="mm_up_dw")
    dx2, g["norm2_g"] = _rms_bwd(dh2, s["x2"], w["norm2_g"], dx3, "rms2_bwd")
    dx2b = dx2.astype(BF16)
    dmix, = _matmul(dx2b, w["w_out"], "nt", [F32], name="mm_out_dx")
    g["w_out"], = _matmul(s["mix"], dx2b, "tn", [F32], name="mm_out_dw")
    dy_gdn, dy_swa, dgg, dgs = _gate_mix_bwd(dmix, s["proj"], s["y_gdn"], s["y_swa"])
    do_gdn, = _matmul(dy_gdn, w["w_branch_gdn"], "nt", [BF16], name="mm_bgdn_dx")
    g["w_branch_gdn"], = _matmul(s["o_gdn"], dy_gdn, "tn", [F32], name="mm_bgdn_dw")
    do_swa, = _matmul(dy_swa, w["w_branch_swa"], "nt", [BF16], name="mm_bswa_dx")
    g["w_branch_swa"], = _matmul(s["o_swa"], dy_swa, "tn", [F32], name="mm_bswa_dw")
    a_rows, b_rows, q_s, k_s, v_s = _branch_inputs(s["proj"])
    dq_s, dk_s, dv_s, dsink = _swa_bwd(q_s, k_s, v_s, w["attn_sinks"].reshape(-1, 1, 1), _to_heads(do_swa, SWA_HEADS))
    g["attn_sinks"] = dsink.reshape(-1)
    dy, dz, da_rows, db_rows, dalog, ddt, dng = _gdn_bwd(
        s["y"], s["proj"], a_rows, b_rows, w["a_log"].reshape(-1, 1, 1), w["dt_bias"].reshape(-1, 1, 1),
        w["gdn_norm_g"].reshape(1, 1, -1), s["states"], do_gdn)
    g["a_log"], g["dt_bias"], g["gdn_norm_g"] = dalog.reshape(-1), ddt.reshape(-1), dng.reshape(-1)
    dqkv, g["conv_w"] = _conv_bwd(dy, s["proj"], w["conv_w"])
    dba = jnp.concatenate([_unrows(db_rows), _unrows(da_rows)], axis=1).astype(BF16)
    dproj = jnp.concatenate(
        [dqkv, dz, dgg, dgs, _from_heads(dq_s), _from_heads(dk_s), _from_heads(dv_s), dba,
         jnp.zeros((T, N_PROJ - BA0 - 2 * GDN_HEADS), BF16)], axis=1)
    dh1, = _matmul(dproj, w["w_in"], "nt", [F32], name="mm_proj_dx")
    g["w_in"], = _matmul(s["h1"], dproj, "tn", [F32], name="mm_proj_dw")
    dx, g["norm1_g"] = _rms_bwd(dh1, s["x"], w["norm1_g"], dx2, "rms1_bwd")
    g["norm1_g"], g["norm2_g"] = g["norm1_g"].reshape(-1), g["norm2_g"].reshape(-1)
    return dx, g


_LAYER_KEYS = ("norm1_g", "w_in", "conv_w", "a_log", "dt_bias", "gdn_norm_g", "attn_sinks", "w_branch_gdn",
               "w_branch_swa", "w_out", "norm2_g", "w_ff_up", "w_ff_down")


def _local_step(x, tgt, weights, final_norm_g):
    def fwd(xc, wl):
        return _layer_fwd(xc, wl)

    x_out, saved = lax.scan(fwd, x, weights)
    loss, dx, dgf = _loss_head(x_out, final_norm_g, tgt)

    def bwd(dxc, ws):
        wl, sl = ws
        return _layer_bwd(dxc, wl, sl)

    dx, grads = lax.scan(bwd, dx, (weights, saved), reverse=True)
    return loss, dx, grads, dgf


def _my_index():
    return 4 * lax.axis_index("x") + 2 * lax.axis_index("y") + lax.axis_index("c")


def _peer(mask):
    x, y, c = lax.axis_index("x"), lax.axis_index("y"), lax.axis_index("c")
    px = 1 - x if mask & 4 else x
    py = 1 - y if mask & 2 else y
    pc = 1 - c if mask & 1 else c
    return (px, py, pc), 4 * px + 2 * py + pc


def _all_gather(shards):
    n = len(shards)

    def body(*refs):
        ins, outs = refs[:n], refs[n:2 * n]
        send_sems, recv_sems, local_sems = refs[2 * n:]
        me = _my_index()
        sibling, sib_idx = _peer(1)
        chips = [_peer(m) for m in (4, 2, 6)]

        def copy(a, k, block, to, src=None):
            return pltpu.make_async_remote_copy(
                src_ref=outs[a].at[block] if src is None else src, dst_ref=outs[a].at[block],
                send_sem=send_sems.at[a, k], recv_sem=recv_sems.at[a, k], device_id=to, device_id_type=MESH)

        mine = [pltpu.make_async_copy(ins[a], outs[a].at[me], local_sems.at[a]) for a in range(n)]
        for cp in mine:
            cp.start()
        first = []
        for a in range(n):
            first.append(copy(a, 0, me, sibling, src=ins[a]))
            first += [copy(a, 1 + j, me, dev, src=ins[a]) for j, (dev, _) in enumerate(chips)]
        for cp in first:
            cp.start()
        passed = []
        for a in range(n):
            for j, (_, idx) in enumerate(chips):
                copy(a, 1 + j, idx, sibling).wait_recv()
                cp = copy(a, 4 + j, idx, sibling)
                cp.start()
                passed.append(cp)
        for a in range(n):
            copy(a, 0, sib_idx, sibling).wait_recv()
            for j, (_, idx) in enumerate(chips):
                copy(a, 4 + j, idx ^ 1, sibling).wait_recv()
        for cp in first + passed:
            cp.wait_send()
        for cp in mine:
            cp.wait()

    any_spec = pl.BlockSpec(memory_space=pl.ANY)
    return pl.pallas_call(
        body, in_specs=[any_spec] * n, out_specs=[any_spec] * n,
        out_shape=[jax.ShapeDtypeStruct((N_DEV,) + s.shape, s.dtype) for s in shards],
        scratch_shapes=[pltpu.SemaphoreType.DMA((n, 7)), pltpu.SemaphoreType.DMA((n, 7)), pltpu.SemaphoreType.DMA((n,))],
        name="weight_all_gather",
    )(*shards)


def _scatter(blocked):
    n = len(blocked)

    def body(*refs):
        ins, outs = refs[:n], refs[n:2 * n]
        send_sems, recv_sems, local_sems = refs[2 * n:]
        me = _my_index()
        mine = [pltpu.make_async_copy(ins[a].at[me], outs[a].at[me], local_sems.at[a]) for a in range(n)]
        for cp in mine:
            cp.start()
        sends = []
        for a in range(n):
            for mask in range(1, N_DEV):
                dev, idx = _peer(mask)
                sends.append(pltpu.make_async_remote_copy(
                    src_ref=ins[a].at[idx], dst_ref=outs[a].at[me], send_sem=send_sems.at[a, mask - 1],
                    recv_sem=recv_sems.at[a, mask - 1], device_id=dev, device_id_type=MESH))
        for cp in sends:
            cp.start()
        for a in range(n):
            for mask in range(1, N_DEV):
                dev, idx = _peer(mask)
                pltpu.make_async_remote_copy(
                    src_ref=ins[a].at[idx], dst_ref=outs[a].at[idx], send_sem=send_sems.at[a, mask - 1],
                    recv_sem=recv_sems.at[a, mask - 1], device_id=dev, device_id_type=MESH).wait_recv()
        for cp in sends:
            cp.wait_send()
        for cp in mine:
            cp.wait()

    any_spec = pl.BlockSpec(memory_space=pl.ANY)
    return pl.pallas_call(
        body, in_specs=[any_spec] * n, out_specs=[any_spec] * n,
        out_shape=[jax.ShapeDtypeStruct(b.shape, b.dtype) for b in blocked],
        scratch_shapes=[pltpu.SemaphoreType.DMA((n, 7)), pltpu.SemaphoreType.DMA((n, 7)), pltpu.SemaphoreType.DMA((n,))],
        name="grad_scatter",
    )(*blocked)


def _all_reduce_small(part):
    R = part.shape[0]

    def body(x_ref, sum_ref, gath, send_sems, recv_sems):
        me = _my_index()
        gath[me] = x_ref[...]
        sends = []
        for mask in range(1, N_DEV):
            dev, _ = _peer(mask)
            sends.append(pltpu.make_async_remote_copy(
                src_ref=x_ref, dst_ref=gath.at[me], send_sem=send_sems.at[mask - 1], recv_sem=recv_sems.at[mask - 1],
                device_id=dev, device_id_type=MESH))
        for cp in sends:
            cp.start()
        for mask in range(1, N_DEV):
            dev, idx = _peer(mask)
            pltpu.make_async_remote_copy(
                src_ref=x_ref, dst_ref=gath.at[idx], send_sem=send_sems.at[mask - 1], recv_sem=recv_sems.at[mask - 1],
                device_id=dev, device_id_type=MESH).wait_recv()
        for cp in sends:
            cp.wait_send()
        acc = gath[0]
        for i in range(1, N_DEV):
            acc = acc + gath[i]
        sum_ref[...] = acc

    vm = pl.BlockSpec(memory_space=pltpu.VMEM)
    return pl.pallas_call(
        body, in_specs=[vm], out_specs=vm, out_shape=jax.ShapeDtypeStruct((R, 128), F32),
        scratch_shapes=[pltpu.VMEM((N_DEV, R, 128), F32), pltpu.SemaphoreType.DMA((7,)), pltpu.SemaphoreType.DMA((7,))],
        name="small_all_reduce",
    )(part)


def _adam_math(w, g, m, v):
    m = ADAM_B1 * m + (1.0 - ADAM_B1) * g
    v = ADAM_B2 * v + (1.0 - ADAM_B2) * (g * g)
    m_hat = m / (1.0 - ADAM_B1 ** ADAM_STEP)
    v_hat = v / (1.0 - ADAM_B2 ** ADAM_STEP)
    delta = -ADAM_LR * (m_hat / (jnp.sqrt(v_hat) + ADAM_EPS) + ADAM_WD * w)
    return delta, m, v


def _adamw_sum(parts, w, m, v, name):
    R, Cc = w.shape
    tr = R
    while tr * Cc * 4 > (1 << 20) and tr % 32 == 0:
        tr //= 2

    def body(p_ref, w_ref, m_ref, v_ref, g_ref, d_ref, nm_ref, nv_ref):
        g = p_ref[0].astype(F32)
        for i in range(1, N_DEV):
            g = g + p_ref[i].astype(F32)
        d, nm, nv = _adam_math(w_ref[...], g, m_ref[...], v_ref[...])
        g_ref[...], d_ref[...], nm_ref[...], nv_ref[...] = g, d, nm, nv

    blk = pl.BlockSpec((tr, Cc), lambda i: (i, 0))
    out = jax.ShapeDtypeStruct((R, Cc), F32)
    return pl.pallas_call(
        body, grid=(R // tr,), in_specs=[pl.BlockSpec((N_DEV, tr, Cc), lambda i: (0, i, 0)), blk, blk, blk],
        out_specs=[blk] * 4, out_shape=[out] * 4, compiler_params=_cp("parallel"), name=name,
    )(parts, w, m, v)


def _adamw_small(g, w, m, v):
    def body(g_ref, w_ref, m_ref, v_ref, d_ref, nm_ref, nv_ref):
        d_ref[...], nm_ref[...], nv_ref[...] = _adam_math(w_ref[...], g_ref[...], m_ref[...], v_ref[...])

    out = jax.ShapeDtypeStruct(g.shape, F32)
    return pl.pallas_call(body, out_shape=[out] * 3, name="adamw_small")(g, w, m, v)


def _align_w_in(w_full):
    out = jnp.zeros(w_full.shape[:2] + (N_PROJ,), w_full.dtype)
    for src, width, dst in _SEGS:
        out = lax.dynamic_update_slice_in_dim(out, w_full[..., src:src + width], dst, axis=2)
    return out


def _unalign_w_in(g):
    return jnp.concatenate([g[..., dst:dst + width] for _, width, dst in _SEGS], axis=-1)


_SHARDED = {"w_in": 1, "conv_w": 1, "w_branch_gdn": 0, "w_branch_swa": 0, "w_out": 0, "w_ff_up": 1, "w_ff_down": 0}
_SMALL = ("norm1_g", "a_log", "dt_bias", "gdn_norm_g", "attn_sinks", "norm2_g", "final_norm_g")


def _unblock(gathered, axis):
    if axis == 0:
        t = gathered.transpose(1, 0, 2, 3)
        return t.reshape(t.shape[0], -1, t.shape[3])
    t = gathered.transpose(1, 2, 0, 3)
    return t.reshape(t.shape[0], t.shape[1], -1)


def _block(full, axis):
    L, A, B = full.shape
    if axis == 0:
        return full.reshape(L, N_DEV, A // N_DEV, B).transpose(1, 0, 2, 3)
    return full.reshape(L, A, N_DEV, B // N_DEV).transpose(2, 0, 1, 3)


def _pack_small(d, loss_row=None):
    rows = [d[k].astype(F32).reshape(-1, 128) if d[k].size % 128 == 0 else
            jnp.pad(d[k].astype(F32), ((0, 0), (0, 128 - d[k].shape[-1]))) for k in _SMALL]
    rows.append(jnp.zeros((1, 128), F32) if loss_row is None else loss_row)
    packed = jnp.concatenate(rows, axis=0)
    return jnp.pad(packed, ((0, -packed.shape[0] % 8), (0, 0)))


def _unpack_small(packed, like):
    out, r = {}, 0
    for k in _SMALL:
        shp = like[k].shape
        if like[k].size % 128 == 0:
            n = like[k].size // 128
            out[k] = packed[r:r + n].reshape(shp)
        else:
            n = shp[0]
            out[k] = packed[r:r + n, :shp[-1]]
        r += n
    return out, packed[r, 0]


def kernel(x, norm1_g, w_in, conv_w, a_log, dt_bias, gdn_norm_g, attn_sinks, w_branch_gdn, w_branch_swa, w_out, norm2_g, w_ff_up, w_ff_down, final_norm_g, loss_target, m_norm1_g, m_w_in, m_conv_w, m_a_log, m_dt_bias, m_gdn_norm_g, m_attn_sinks, m_w_branch_gdn, m_w_branch_swa, m_w_out, m_norm2_g, m_w_ff_up, m_w_ff_down, m_final_norm_g, v_norm1_g, v_w_in, v_conv_w, v_a_log, v_dt_bias, v_gdn_norm_g, v_attn_sinks, v_w_branch_gdn, v_w_branch_swa, v_w_out, v_norm2_g, v_w_ff_up, v_w_ff_down, v_final_norm_g):
    names = ("norm1_g", "w_in", "conv_w", "a_log", "dt_bias", "gdn_norm_g", "attn_sinks", "w_branch_gdn",
             "w_branch_swa", "w_out", "norm2_g", "w_ff_up", "w_ff_down", "final_norm_g")
    w = dict(zip(names, (norm1_g, w_in, conv_w, a_log, dt_bias, gdn_norm_g, attn_sinks, w_branch_gdn, w_branch_swa,
                         w_out, norm2_g, w_ff_up, w_ff_down, final_norm_g)))
    m = dict(zip(names, (m_norm1_g, m_w_in, m_conv_w, m_a_log, m_dt_bias, m_gdn_norm_g, m_attn_sinks, m_w_branch_gdn,
                         m_w_branch_swa, m_w_out, m_norm2_g, m_w_ff_up, m_w_ff_down, m_final_norm_g)))
    v = dict(zip(names, (v_norm1_g, v_w_in, v_conv_w, v_a_log, v_dt_bias, v_gdn_norm_g, v_attn_sinks, v_w_branch_gdn,
                         v_w_branch_swa, v_w_out, v_norm2_g, v_w_ff_up, v_w_ff_down, v_final_norm_g)))
    sharded = list(_SHARDED)

    gathered = _all_gather([w[k] if k == "conv_w" else w[k].astype(BF16) for k in sharded])
    full = {k: _unblock(gth, _SHARDED[k]) for k, gth in zip(sharded, gathered)}
    full["w_in"] = _align_w_in(full["w_in"])
    layer_w = {k: (full[k] if k in full else w[k]) for k in _LAYER_KEYS}

    loss_row, grad_x, grads, dgf = _local_step(x[0], loss_target[0], layer_w, final_norm_g)

    grads["w_in"] = _unalign_w_in(grads["w_in"])
    landed = _scatter([_block(grads[k], _SHARDED[k]).astype(BF16) for k in sharded])
    out_g, out_d, out_m, out_v = {}, {}, {}, {}
    for k, parts in zip(sharded, landed):
        shp = w[k].shape
        flat = lambda t: t.reshape(-1, shp[-1])
        res = _adamw_sum(parts.reshape(N_DEV, -1, shp[-1]), flat(w[k]), flat(m[k]), flat(v[k]), "adamw_" + k)
        out_g[k], out_d[k], out_m[k], out_v[k] = [t.reshape(shp) for t in res]

    small_g = dict(grads)
    small_g["final_norm_g"] = dgf.reshape(-1)
    total = _all_reduce_small(_pack_small(small_g, loss_row))
    sd, sm, sv = _adamw_small(total, _pack_small(w), _pack_small(m), _pack_small(v))
    g_small, loss = _unpack_small(total, w)
    d_small, _ = _unpack_small(sd, w)
    m_small, _ = _unpack_small(sm, w)
    v_small, _ = _unpack_small(sv, w)
    out_g.update(g_small), out_d.update(d_small), out_m.update(m_small), out_v.update(v_small)

    return (loss, grad_x[None], *[out_g[k] for k in names], *[out_d[k] for k in names],
            *[out_m[k] for k in names], *[out_v[k] for k in names])
```

```python
import functools
import math

import jax
import jax.numpy as jnp
from jax import lax
from jax.experimental import pallas as pl
from jax.experimental.pallas import tpu as pltpu

F32 = jnp.float32
BF16 = jnp.bfloat16
HI = lax.Precision.HIGHEST
MESH = pl.DeviceIdType.MESH

N_DEV = 8
D_MODEL = 2048
DEPTH = 4
GDN_HEADS = 16
HEAD_DIM = 128
CHUNK = 64
CONV_K = 4
SWA_HEADS = 32
SWA_KV = 4
SWA_GROUP = SWA_HEADS // SWA_KV
SWA_DIM = 64
WINDOW = 128
D_FF = 4 * D_MODEL
D_IN = 14880
NORM_EPS = 1e-6
GDN_W = GDN_HEADS * HEAD_DIM
KV_W = SWA_KV * SWA_DIM

QKV0, Z0, GG0, GS0, QS0, KS0, VS0, BA0 = 0, 6144, 8192, 10240, 12288, 14336, 14592, 14848
N_PROJ = 15360
_SEGS = ((0, 6144, QKV0), (6144, 2048, Z0), (8192, 32, BA0), (8224, 2048, QS0), (10272, 256, KS0),
         (10528, 256, VS0), (10784, 4096, GG0))

ADAM_LR, ADAM_B1, ADAM_B2, ADAM_EPS, ADAM_WD, ADAM_STEP = 0.001, 0.9, 0.999, 1e-08, 0.01, 10

VMEM_LIMIT = 56 * 1024 * 1024
GDN_HB = 4
GDN_PREC_SOLVE = "x3"
GDN_PREC = "bf16"
SWA_PREC = "bf16"


def _cp(*sem):
    return pltpu.CompilerParams(dimension_semantics=sem, vmem_limit_bytes=VMEM_LIMIT)


def _tile(n, want):
    t = min(n, want)
    while n % t:
        t -= 128
    return t


def _matmul(a, b, mode, out_shapes, epilogue=None, extras=(), name="mm"):
    if mode == "tn":
        K, M = a.shape
    else:
        M, K = a.shape
    N = b.shape[0] if mode == "nt" else b.shape[1]
    tm, tn, tk = _tile(M, 1024), _tile(N, 1024), _tile(K, 512)
    nk = K // tk
    dims = {"nn": (((1,), (0,)), ((), ())), "nt": (((1,), (1,)), ((), ())), "tn": (((0,), (0,)), ((), ()))}[mode]
    n_ex, n_out = len(extras), len(out_shapes)

    def body(*refs):
        a_ref, b_ref = refs[:2]
        ex = refs[2:2 + n_ex]
        outs = refs[2 + n_ex:2 + n_ex + n_out]
        acc = refs[-1]
        k = pl.program_id(2)

        @pl.when(k == 0)
        def _():
            acc[...] = jnp.zeros_like(acc)

        acc[...] += lax.dot_general(a_ref[...].astype(BF16), b_ref[...].astype(BF16), dims,
                                    preferred_element_type=F32)

        @pl.when(k == nk - 1)
        def _():
            res = (acc[...],) if epilogue is None else epilogue(acc[...], *[e[...] for e in ex])
            for o, r in zip(outs, res):
                o[...] = r.astype(o.dtype)

    a_spec = pl.BlockSpec((tk, tm), lambda i, j, k: (k, i)) if mode == "tn" else pl.BlockSpec((tm, tk), lambda i, j, k: (i, k))
    b_spec = pl.BlockSpec((tn, tk), lambda i, j, k: (j, k)) if mode == "nt" else pl.BlockSpec((tk, tn), lambda i, j, k: (k, j))
    mn = pl.BlockSpec((tm, tn), lambda i, j, k: (i, j))
    return pl.pallas_call(
        body, grid=(M // tm, N // tn, nk),
        in_specs=[a_spec, b_spec] + [mn] * n_ex, out_specs=[mn] * n_out,
        out_shape=[jax.ShapeDtypeStruct((M, N), dt) for dt in out_shapes],
        scratch_shapes=[pltpu.VMEM((tm, tn), F32)],
        compiler_params=_cp("parallel", "parallel", "arbitrary"), name=name,
    )(a, b, *extras)


def _rms_fwd(x, g, name):
    T, D = x.shape
    tr = _tile(T, 256)

    def body(x_ref, g_ref, h_ref):
        xv = x_ref[...]
        r = lax.rsqrt(jnp.mean(xv * xv, axis=-1, keepdims=True) + NORM_EPS)
        h_ref[...] = (xv * r * g_ref[...]).astype(BF16)

    return pl.pallas_call(
        body, grid=(T // tr,),
        in_specs=[pl.BlockSpec((tr, D), lambda i: (i, 0)), pl.BlockSpec((1, D), lambda i: (0, 0))],
        out_specs=pl.BlockSpec((tr, D), lambda i: (i, 0)),
        out_shape=jax.ShapeDtypeStruct((T, D), BF16), compiler_params=_cp("parallel"), name=name,
    )(x, g.reshape(1, D))


def _rms_bwd(dh, x, g, dres, name):
    T, D = x.shape
    tr = _tile(T, 256)

    def body(dh_ref, x_ref, g_ref, dres_ref, dx_ref, dg_ref):
        @pl.when(pl.program_id(0) == 0)
        def _():
            dg_ref[...] = jnp.zeros_like(dg_ref)

        xv = x_ref[...]
        r = lax.rsqrt(jnp.mean(xv * xv, axis=-1, keepdims=True) + NORM_EPS)
        xhat = xv * r
        dhv = dh_ref[...].astype(F32)
        gd = dhv * g_ref[...]
        dx_ref[...] = dres_ref[...] + r * (gd - xhat * jnp.mean(gd * xhat, axis=-1, keepdims=True))
        dg_ref[...] += jnp.sum(dhv * xhat, axis=0, keepdims=True)

    row = pl.BlockSpec((tr, D), lambda i: (i, 0))
    vec = pl.BlockSpec((1, D), lambda i: (0, 0))
    return pl.pallas_call(
        body, grid=(T // tr,), in_specs=[row, row, vec, row], out_specs=[row, vec],
        out_shape=[jax.ShapeDtypeStruct((T, D), F32), jax.ShapeDtypeStruct((1, D), F32)],
        compiler_params=_cp("arbitrary"), name=name,
    )(dh, x, g.reshape(1, D), dres)


def _loss_head(x, g, tgt):
    T, D = x.shape
    tr = _tile(T, 256)

    def body(x_ref, g_ref, t_ref, loss_ref, dx_ref, dg_ref):
        @pl.when(pl.program_id(0) == 0)
        def _():
            dg_ref[...] = jnp.zeros_like(dg_ref)
            loss_ref[...] = jnp.zeros_like(loss_ref)

        xv = x_ref[...]
        r = lax.rsqrt(jnp.mean(xv * xv, axis=-1, keepdims=True) + NORM_EPS)
        xhat = xv * r
        err = xhat * g_ref[...] - t_ref[...]
        loss_ref[...] += (0.5 / D) * jnp.sum(jnp.sum(err * err, axis=-1, keepdims=True), axis=0, keepdims=True)
        dy = err * (1.0 / D)
        gd = dy * g_ref[...]
        dx_ref[...] = r * (gd - xhat * jnp.mean(gd * xhat, axis=-1, keepdims=True))
        dg_ref[...] += jnp.sum(dy * xhat, axis=0, keepdims=True)

    row = pl.BlockSpec((tr, D), lambda i: (i, 0))
    vec = pl.BlockSpec((1, D), lambda i: (0, 0))
    return pl.pallas_call(
        body, grid=(T // tr,), in_specs=[row, vec, row],
        out_specs=[pl.BlockSpec((1, 128), lambda i: (0, 0)), row, vec],
        out_shape=[jax.ShapeDtypeStruct((1, 128), F32), jax.ShapeDtypeStruct((T, D), F32),
                   jax.ShapeDtypeStruct((1, D), F32)],
        compiler_params=_cp("arbitrary"), name="loss_head",
    )(x, g.reshape(1, D), tgt)


def _sigmoid(x):
    return 1.0 / (1.0 + jnp.exp(-x))


def _gate_mix_fwd(proj, y_gdn, y_swa):
    T = proj.shape[0]
    tr, tc = _tile(T, 512), 512
    nc = D_MODEL // tc

    def body(gg_ref, gs_ref, yg_ref, ys_ref, mix_ref):
        mix_ref[...] = (_sigmoid(gg_ref[...]) * yg_ref[...] + _sigmoid(gs_ref[...]) * ys_ref[...]).astype(BF16)

    blk = pl.BlockSpec((tr, tc), lambda i, j: (i, j))
    return pl.pallas_call(
        body, grid=(T // tr, nc),
        in_specs=[pl.BlockSpec((tr, tc), lambda i, j: (i, GG0 // tc + j)),
                  pl.BlockSpec((tr, tc), lambda i, j: (i, GS0 // tc + j)), blk, blk],
        out_specs=blk, out_shape=jax.ShapeDtypeStruct((T, D_MODEL), BF16),
        compiler_params=_cp("parallel", "parallel"), name="gate_mix_fwd",
    )(proj, proj, y_gdn, y_swa)


def _gate_mix_bwd(dmix, proj, y_gdn, y_swa):
    T = proj.shape[0]
    tr, tc = _tile(T, 512), 512
    nc = D_MODEL // tc

    def body(dm_ref, gg_ref, gs_ref, yg_ref, ys_ref, dyg_ref, dys_ref, dgg_ref, dgs_ref):
        dm = dm_ref[...]
        sg, ss = _sigmoid(gg_ref[...]), _sigmoid(gs_ref[...])
        dyg_ref[...] = (dm * sg).astype(BF16)
        dys_ref[...] = (dm * ss).astype(BF16)
        dgg_ref[...] = (dm * yg_ref[...] * sg * (1.0 - sg)).astype(BF16)
        dgs_ref[...] = (dm * ys_ref[...] * ss * (1.0 - ss)).astype(BF16)

    blk = pl.BlockSpec((tr, tc), lambda i, j: (i, j))
    out = jax.ShapeDtypeStruct((T, D_MODEL), BF16)
    return pl.pallas_call(
        body, grid=(T // tr, nc),
        in_specs=[blk, pl.BlockSpec((tr, tc), lambda i, j: (i, GG0 // tc + j)),
                  pl.BlockSpec((tr, tc), lambda i, j: (i, GS0 // tc + j)), blk, blk],
        out_specs=[blk] * 4, out_shape=[out] * 4,
        compiler_params=_cp("parallel", "parallel"), name="gate_mix_bwd",
    )(dmix, proj, proj, y_gdn, y_swa)


CONV_TT, CONV_TW, HALO = 512, 256, 8


def _conv_fwd(proj, conv_w):
    T = proj.shape[0]
    tt = _tile(T, CONV_TT)
    W = 3 * GDN_W

    def body(x_ref, halo_ref, w_ref, y_ref):
        first = pl.program_id(1) == 0
        halo = jnp.where(first, 0.0, halo_ref[...])
        xe = jnp.concatenate([halo, x_ref[...]], axis=0)
        acc = xe[HALO:] * w_ref[pl.ds(CONV_K - 1, 1), :]
        for j in range(CONV_K - 1):
            acc = acc + pltpu.roll(xe, CONV_K - 1 - j, 0)[HALO:] * w_ref[pl.ds(j, 1), :]
        y_ref[...] = acc

    return pl.pallas_call(
        body, grid=(W // CONV_TW, T // tt),
        in_specs=[pl.BlockSpec((tt, CONV_TW), lambda c, t: (t, c)),
                  pl.BlockSpec((HALO, CONV_TW), lambda c, t: (jnp.maximum(t * (tt // HALO) - 1, 0), c)),
                  pl.BlockSpec((CONV_K, CONV_TW), lambda c, t: (0, c))],
        out_specs=pl.BlockSpec((tt, CONV_TW), lambda c, t: (t, c)),
        out_shape=jax.ShapeDtypeStruct((T, W), F32),
        compiler_params=_cp("parallel", "parallel"), name="conv_fwd",
    )(proj, proj, conv_w)


def _conv_bwd(dy, proj, conv_w):
    T = proj.shape[0]
    tt = _tile(T, CONV_TT)
    nt = T // tt
    W = 3 * GDN_W

    def body(dy_ref, dnext_ref, x_ref, halo_ref, w_ref, dx_ref, dw_ref):
        t = pl.program_id(1)

        @pl.when(t == 0)
        def _():
            dw_ref[...] = jnp.zeros_like(dw_ref)

        dyv = dy_ref[...]
        dye = jnp.concatenate([dyv, jnp.where(t == nt - 1, 0.0, dnext_ref[...])], axis=0)
        xe = jnp.concatenate([jnp.where(t == 0, 0.0, halo_ref[...]), x_ref[...]], axis=0)
        acc = dyv * w_ref[pl.ds(CONV_K - 1, 1), :]
        dw_ref[pl.ds(CONV_K - 1, 1), :] += jnp.sum(dyv * xe[HALO:], axis=0, keepdims=True)
        for s in range(1, CONV_K):
            j = CONV_K - 1 - s
            acc = acc + pltpu.roll(dye, tt + HALO - s, 0)[:tt] * w_ref[pl.ds(j, 1), :]
            dw_ref[pl.ds(j, 1), :] += jnp.sum(dyv * pltpu.roll(xe, s, 0)[HALO:], axis=0, keepdims=True)
        dx_ref[...] = acc.astype(BF16)

    cur = pl.BlockSpec((tt, CONV_TW), lambda c, t: (t, c))
    return pl.pallas_call(
        body, grid=(W // CONV_TW, nt),
        in_specs=[cur,
                  pl.BlockSpec((HALO, CONV_TW), lambda c, t: (jnp.minimum((t + 1) * (tt // HALO), T // HALO - 1), c)),
                  cur,
                  pl.BlockSpec((HALO, CONV_TW), lambda c, t: (jnp.maximum(t * (tt // HALO) - 1, 0), c)),
                  pl.BlockSpec((CONV_K, CONV_TW), lambda c, t: (0, c))],
        out_specs=[cur, pl.BlockSpec((CONV_K, CONV_TW), lambda c, t: (0, c))],
        out_shape=[jax.ShapeDtypeStruct((T, W), BF16), jax.ShapeDtypeStruct((CONV_K, W), F32)],
        compiler_params=_cp("parallel", "arbitrary"), name="conv_bwd",
    )(dy, dy, proj, proj, conv_w)


def _dot(a, b, kind, prec):
    nb = a.ndim - 2
    batch = tuple(range(nb))
    ca = nb if kind == "tn" else nb + 1
    cb = nb + 1 if kind == "nt" else nb
    dims = (((ca,), (cb,)), (batch, batch))
    if prec == "bf16":
        return lax.dot_general(a.astype(BF16), b.astype(BF16), dims, preferred_element_type=F32)
    return lax.dot_general(a, b, dims, precision=HI if prec == "f32" else lax.Precision.HIGH,
                           preferred_element_type=F32)


@functools.lru_cache(maxsize=None)
def _mm(kind, prec):
    @jax.custom_vjp
    def f(a, b):
        return _dot(a, b, kind, prec)

    def fwd(a, b):
        return f(a, b), (a, b)

    def bwd(res, ct):
        a, b = res
        if kind == "nn":
            return _dot(ct, b, "nt", prec), _dot(a, ct, "tn", prec)
        if kind == "nt":
            return _dot(ct, b, "nn", prec), _dot(ct, a, "tn", prec)
        return _dot(b, ct, "nt", prec), _dot(a, ct, "nn", prec)

    f.defvjp(fwd, bwd)
    return f


def _col_from_row(row, eye):
    return jnp.sum(jnp.where(eye, row, 0.0), axis=2, keepdims=True)


def _silu(x):
    return x / (1.0 + jnp.exp(-x))


def _gdn_chunk(yq, yk, yv, z, a_row, b_row, a_log, dt_bias, norm_g, state):
    h = yq.shape[0]
    ii = lax.broadcasted_iota(jnp.int32, (1, CHUNK, CHUNK), 1)
    jj = lax.broadcasted_iota(jnp.int32, (1, CHUNK, CHUNK), 2)
    eye = ii == jj
    qr, kr, v = _silu(yq), _silu(yk), _silu(yv)
    q = qr * lax.rsqrt(jnp.sum(qr * qr, axis=-1, keepdims=True) + NORM_EPS) * (HEAD_DIM ** -0.5)
    k = kr * lax.rsqrt(jnp.sum(kr * kr, axis=-1, keepdims=True) + NORM_EPS)
    beta_row = _sigmoid(b_row)
    xa = a_row + dt_bias
    g_row = -jnp.exp(a_log) * (jnp.maximum(xa, 0.0) + jnp.log(1.0 + jnp.exp(-jnp.abs(xa))))
    upper = jnp.broadcast_to(jnp.where(ii <= jj, 1.0, 0.0).astype(F32), (h, CHUNK, CHUNK))
    decay_row = _mm("nn", "f32")(g_row, upper)
    decay_col = _col_from_row(decay_row, eye)
    beta_col = _col_from_row(beta_row, eye)
    decay_last = jnp.sum(g_row, axis=2, keepdims=True)
    gamma = jnp.exp(jnp.where(ii >= jj, decay_col - decay_row, -jnp.inf))
    k_beta = k * beta_col
    a_low = jnp.where(ii > jj, _mm("nt", GDN_PREC_SOLVE)(k_beta, k) * gamma, 0.0)
    t_inv = jnp.where(eye, 1.0, 0.0).astype(F32) - a_low
    p = a_low
    for _ in range(5):
        p = _mm("nn", GDN_PREC_SOLVE)(p, p)
        t_inv = t_inv + _mm("nn", GDN_PREC_SOLVE)(t_inv, p)
    e_dec = jnp.exp(decay_col)
    u = _mm("nn", GDN_PREC_SOLVE)(t_inv, v * beta_col)
    w = _mm("nn", GDN_PREC_SOLVE)(t_inv, k_beta * e_dec)
    qk = _mm("nt", GDN_PREC)(q, k) * gamma
    v_new = u - _mm("nn", GDN_PREC)(w, state)
    o = _mm("nn", GDN_PREC)(q * e_dec, state) + _mm("nn", GDN_PREC)(qk, v_new)
    s_new = state * jnp.exp(decay_last) + _mm("tn", GDN_PREC)(k * jnp.exp(decay_last - decay_col), v_new)
    o_n = o * lax.rsqrt(jnp.mean(o * o, axis=-1, keepdims=True) + NORM_EPS) * norm_g
    return o_n * _silu(z), s_new


def _heads(ref, hb):
    return jnp.stack([ref[:, HEAD_DIM * i:HEAD_DIM * (i + 1)] for i in range(hb)], axis=0)


def _gdn_fwd(y, proj, a_rows, b_rows, a_log, dt_bias, norm_g):
    T = y.shape[0]
    H, HB = GDN_HEADS, GDN_HB
    NC, HG, BW = T // CHUNK, GDN_HEADS // GDN_HB, GDN_HB * HEAD_DIM

    def body(q_ref, k_ref, v_ref, z_ref, a_ref, b_ref, alog_ref, dt_ref, ng_ref, o_ref, sst_ref, s_scr):
        @pl.when(pl.program_id(1) == 0)
        def _():
            s_scr[...] = jnp.zeros_like(s_scr)

        state = s_scr[...]
        sst_ref[...] = state
        o_g, s_new = _gdn_chunk(_heads(q_ref, HB), _heads(k_ref, HB), _heads(v_ref, HB), _heads(z_ref, HB),
                                a_ref[...], b_ref[...], alog_ref[...], dt_ref[...], ng_ref[...], state)
        s_scr[...] = s_new
        for i in range(HB):
            o_ref[:, HEAD_DIM * i:HEAD_DIM * (i + 1)] = o_g[i].astype(BF16)

    col = lambda off: pl.BlockSpec((CHUNK, BW), lambda hg, n, off=off: (n, off + hg))
    row = pl.BlockSpec((HB, None, 1, CHUNK), lambda hg, n: (hg, n, 0, 0))
    sc = pl.BlockSpec((HB, 1, 1), lambda hg, n: (hg, 0, 0))
    return pl.pallas_call(
        body, grid=(HG, NC),
        in_specs=[col(0), col(HG), col(2 * HG), col(Z0 // BW), row, row, sc, sc,
                  pl.BlockSpec((1, 1, HEAD_DIM), lambda hg, n: (0, 0, 0))],
        out_specs=[col(0), pl.BlockSpec((HB, None, HEAD_DIM, HEAD_DIM), lambda hg, n: (hg, n, 0, 0))],
        out_shape=[jax.ShapeDtypeStruct((T, GDN_W), BF16), jax.ShapeDtypeStruct((H, NC, HEAD_DIM, HEAD_DIM), F32)],
        scratch_shapes=[pltpu.VMEM((HB, HEAD_DIM, HEAD_DIM), F32)],
        compiler_params=_cp("parallel", "arbitrary"), name="gdn_fwd",
    )(y, y, y, proj, a_rows, b_rows, a_log, dt_bias, norm_g)


def _gdn_bwd(y, proj, a_rows, b_rows, a_log, dt_bias, norm_g, states, do):
    T = y.shape[0]
    H, HB = GDN_HEADS, GDN_HB
    NC, HG, BW = T // CHUNK, GDN_HEADS // GDN_HB, GDN_HB * HEAD_DIM

    def body(q_ref, k_ref, v_ref, z_ref, a_ref, b_ref, alog_ref, dt_ref, ng_ref, sst_ref, do_ref,
             dq_ref, dk_ref, dv_ref, dz_ref, da_ref, db_ref, dalog_ref, ddt_ref, dng_ref, ds_scr):
        hg, n = pl.program_id(0), pl.program_id(1)

        @pl.when(n == 0)
        def _():
            ds_scr[...] = jnp.zeros_like(ds_scr)
            dalog_ref[...] = jnp.zeros_like(dalog_ref)
            ddt_ref[...] = jnp.zeros_like(ddt_ref)

        @pl.when((n == 0) & (hg == 0))
        def _():
            dng_ref[...] = jnp.zeros_like(dng_ref)

        args = (_heads(q_ref, HB), _heads(k_ref, HB), _heads(v_ref, HB), _heads(z_ref, HB), a_ref[...], b_ref[...],
                alog_ref[...], dt_ref[...], ng_ref[...], sst_ref[...])
        _, vjp = jax.vjp(_gdn_chunk, *args)
        dq, dk, dv, dz, da, db, dalog, ddt, dng, d_state = vjp((_heads(do_ref, HB).astype(F32), ds_scr[...]))
        ds_scr[...] = d_state
        for i in range(HB):
            sl = slice(HEAD_DIM * i, HEAD_DIM * (i + 1))
            dq_ref[:, sl] = dq[i]
            dk_ref[:, sl] = dk[i]
            dv_ref[:, sl] = dv[i]
            dz_ref[:, sl] = dz[i].astype(BF16)
        da_ref[...] = da
        db_ref[...] = db
        dalog_ref[...] += dalog
        ddt_ref[...] += ddt
        dng_ref[...] += dng

    rev = lambda n: NC - 1 - n
    col = lambda off: pl.BlockSpec((CHUNK, BW), lambda hg, n, off=off: (rev(n), off + hg))
    row = pl.BlockSpec((HB, None, 1, CHUNK), lambda hg, n: (hg, rev(n), 0, 0))
    sc = pl.BlockSpec((HB, 1, 1), lambda hg, n: (hg, 0, 0))
    ng = pl.BlockSpec((1, 1, HEAD_DIM), lambda hg, n: (0, 0, 0))
    sst = pl.BlockSpec((HB, None, HEAD_DIM, HEAD_DIM), lambda hg, n: (hg, rev(n), 0, 0))
    rw = jax.ShapeDtypeStruct((H, NC, 1, CHUNK), F32)
    s1 = jax.ShapeDtypeStruct((H, 1, 1), F32)
    dq, dk, dv, dz, da, db, dalog, ddt, dng = pl.pallas_call(
        body, grid=(HG, NC),
        in_specs=[col(0), col(HG), col(2 * HG), col(Z0 // BW), row, row, sc, sc, ng, sst, col(0)],
        out_specs=[col(0), col(0), col(0), col(0), row, row, sc, sc, ng],
        out_shape=[jax.ShapeDtypeStruct((T, GDN_W), F32)] * 3 + [jax.ShapeDtypeStruct((T, GDN_W), BF16), rw, rw, s1, s1,
                                                                jax.ShapeDtypeStruct((1, 1, HEAD_DIM), F32)],
        scratch_shapes=[pltpu.VMEM((HB, HEAD_DIM, HEAD_DIM), F32)],
        compiler_params=_cp("arbitrary", "arbitrary"), name="gdn_bwd",
    )(y, y, y, proj, a_rows, b_rows, a_log, dt_bias, norm_g, states, do)
    return jnp.concatenate([dq, dk, dv], axis=1), dz, da, db, dalog, ddt, dng


def _swa_block(q, k_prev, k_cur, v_prev, v_cur, sink, slope, has_prev):
    kb = jnp.concatenate([k_prev, k_cur], axis=0)
    vb = jnp.concatenate([v_prev, v_cur], axis=0)
    q2 = q.reshape(SWA_GROUP * WINDOW, SWA_DIM)
    s = _mm("nt", SWA_PREC)(q2, kb)
    s = s.reshape(SWA_GROUP, WINDOW, 2 * WINDOW) * (SWA_DIM ** -0.5)
    qi = lax.broadcasted_iota(jnp.int32, (1, WINDOW, 2 * WINDOW), 1)
    sj = lax.broadcasted_iota(jnp.int32, (1, WINDOW, 2 * WINDOW), 2)
    dist = qi + WINDOW - sj
    valid = (dist >= 0) & (dist < WINDOW) & (has_prev | (sj >= WINDOW))
    s = jnp.where(valid, s - slope * dist.astype(F32), -jnp.inf)
    m = lax.stop_gradient(jnp.maximum(jnp.max(s, axis=-1, keepdims=True), sink))
    p = jnp.exp(s - m)
    probs = p / (jnp.sum(p, axis=-1, keepdims=True) + jnp.exp(sink - m))
    o = _mm("nn", SWA_PREC)(probs.reshape(SWA_GROUP * WINDOW, 2 * WINDOW), vb)
    return o.reshape(SWA_GROUP, WINDOW, SWA_DIM)


def _alibi_slopes():
    return (2.0 ** (-8.0 * jnp.arange(1, SWA_HEADS + 1, dtype=F32) / SWA_HEADS)).reshape(SWA_HEADS, 1, 1)


def _swa_fwd(q, k, v, sinks):
    T = q.shape[1]
    NB = T // WINDOW

    def body(q_ref, kp_ref, kc_ref, vp_ref, vc_ref, sink_ref, slope_ref, o_ref):
        o = _swa_block(q_ref[...], kp_ref[...], kc_ref[...], vp_ref[...], vc_ref[...], sink_ref[...],
                       slope_ref[...], pl.program_id(1) > 0)
        o_ref[...] = o.astype(BF16)

    qs = pl.BlockSpec((SWA_GROUP, WINDOW, SWA_DIM), lambda h, n: (h, n, 0))
    cur = pl.BlockSpec((None, WINDOW, SWA_DIM), lambda h, n: (h, n, 0))
    prev = pl.BlockSpec((None, WINDOW, SWA_DIM), lambda h, n: (h, jnp.maximum(n - 1, 0), 0))
    hs = pl.BlockSpec((SWA_GROUP, 1, 1), lambda h, n: (h, 0, 0))
    return pl.pallas_call(
        body, grid=(SWA_KV, NB), in_specs=[qs, prev, cur, prev, cur, hs, hs], out_specs=qs,
        out_shape=jax.ShapeDtypeStruct((SWA_HEADS, T, SWA_DIM), BF16),
        compiler_params=_cp("parallel", "parallel"), name="swa_fwd",
    )(q, k, k, v, v, sinks, _alibi_slopes())


def _swa_bwd(q, k, v, sinks, do):
    T = q.shape[1]
    NB = T // WINDOW

    def body(q_ref, kp_ref, kc_ref, vp_ref, vc_ref, sink_ref, slope_ref, do_ref,
             dq_ref, dk_ref, dv_ref, dsink_ref, dk_scr, dv_scr):
        n = pl.program_id(1)

        @pl.when(n == 0)
        def _():
            dk_scr[...] = jnp.zeros_like(dk_scr)
            dv_scr[...] = jnp.zeros_like(dv_scr)
            dsink_ref[...] = jnp.zeros_like(dsink_ref)

        has_prev = n < NB - 1
        fn = functools.partial(_swa_block, slope=slope_ref[...], has_prev=has_prev)
        _, vjp = jax.vjp(fn, q_ref[...], kp_ref[...], kc_ref[...], vp_ref[...], vc_ref[...], sink_ref[...])
        dq, dkp, dkc, dvp, dvc, dsink = vjp(do_ref[...].astype(F32))
        dq_ref[...] = dq.astype(BF16)
        dk_ref[...] = (dkc + dk_scr[...]).astype(BF16)
        dv_ref[...] = (dvc + dv_scr[...]).astype(BF16)
        dk_scr[...] = dkp
        dv_scr[...] = dvp
        dsink_ref[...] += dsink

    rev = lambda n: NB - 1 - n
    qs = pl.BlockSpec((SWA_GROUP, WINDOW, SWA_DIM), lambda h, n: (h, rev(n), 0))
    cur = pl.BlockSpec((None, WINDOW, SWA_DIM), lambda h, n: (h, rev(n), 0))
    prev = pl.BlockSpec((None, WINDOW, SWA_DIM), lambda h, n: (h, jnp.maximum(rev(n) - 1, 0), 0))
    hs = pl.BlockSpec((SWA_GROUP, 1, 1), lambda h, n: (h, 0, 0))
    kv = jax.ShapeDtypeStruct((SWA_KV, T, SWA_DIM), BF16)
    return pl.pallas_call(
        body, grid=(SWA_KV, NB), in_specs=[qs, prev, cur, prev, cur, hs, hs, qs],
        out_specs=[qs, cur, cur, hs],
        out_shape=[jax.ShapeDtypeStruct((SWA_HEADS, T, SWA_DIM), BF16), kv, kv,
                   jax.ShapeDtypeStruct((SWA_HEADS, 1, 1), F32)],
        scratch_shapes=[pltpu.VMEM((WINDOW, SWA_DIM), F32), pltpu.VMEM((WINDOW, SWA_DIM), F32)],
        compiler_params=_cp("parallel", "arbitrary"), name="swa_bwd",
    )(q, k, k, v, v, sinks, _alibi_slopes(), do)


def _to_heads(t, n_heads):
    return t.reshape(t.shape[0], n_heads, SWA_DIM).transpose(1, 0, 2)


def _from_heads(t):
    return t.transpose(1, 0, 2).reshape(t.shape[1], -1)


def _rows(t):
    return t.T.reshape(t.shape[1], t.shape[0] // CHUNK, 1, CHUNK)


def _unrows(t):
    return t.reshape(t.shape[0], -1).T


def _branch_inputs(proj):
    b_rows = _rows(proj[:, BA0:BA0 + GDN_HEADS])
    a_rows = _rows(proj[:, BA0 + GDN_HEADS:BA0 + 2 * GDN_HEADS])
    q_s = _to_heads(proj[:, QS0:QS0 + D_MODEL], SWA_HEADS)
    k_s = _to_heads(proj[:, KS0:KS0 + KV_W], SWA_KV)
    v_s = _to_heads(proj[:, VS0:VS0 + KV_W], SWA_KV)
    return a_rows, b_rows, q_s, k_s, v_s


def _relu2_epilogue(acc):
    r = jnp.maximum(acc, 0.0)
    return acc, r * r


def _layer_fwd(x, w):
    h1 = _rms_fwd(x, w["norm1_g"], "rms1_fwd")
    proj, = _matmul(h1, w["w_in"], "nn", [F32], name="mm_proj")
    a_rows, b_rows, q_s, k_s, v_s = _branch_inputs(proj)
    y = _conv_fwd(proj, w["conv_w"])
    o_gdn, states = _gdn_fwd(y, proj, a_rows, b_rows, w["a_log"].reshape(-1, 1, 1), w["dt_bias"].reshape(-1, 1, 1),
                             w["gdn_norm_g"].reshape(1, 1, -1))
    o_swa = _from_heads(_swa_fwd(q_s, k_s, v_s, w["attn_sinks"].reshape(-1, 1, 1)))
    y_gdn, = _matmul(o_gdn, w["w_branch_gdn"], "nn", [F32], name="mm_bgdn")
    y_swa, = _matmul(o_swa, w["w_branch_swa"], "nn", [F32], name="mm_bswa")
    mix = _gate_mix_fwd(proj, y_gdn, y_swa)
    x2, = _matmul(mix, w["w_out"], "nn", [F32], epilogue=lambda acc, r: (r + acc,), extras=(x,), name="mm_out")
    h2 = _rms_fwd(x2, w["norm2_g"], "rms2_fwd")
    u, act = _matmul(h2, w["w_ff_up"], "nn", [F32, BF16], epilogue=_relu2_epilogue, name="mm_up")
    x3, = _matmul(act, w["w_ff_down"], "nn", [F32], epilogue=lambda acc, r: (r + acc,), extras=(x2,), name="mm_down")
    saved = dict(x=x, h1=h1, proj=proj, y=y, states=states, o_gdn=o_gdn, o_swa=o_swa, y_gdn=y_gdn, y_swa=y_swa,
                 mix=mix, x2=x2, h2=h2, u=u, act=act)
    return x3, saved


def _layer_bwd(dx3, w, s):
    T = dx3.shape[0]
    g = {}
    dx3b = dx3.astype(BF16)
    du, = _matmul(dx3b, w["w_ff_down"], "nt", [BF16], extras=(s["u"],),
                  epilogue=lambda acc, uu: (acc * 2.0 * jnp.maximum(uu, 0.0),), name="mm_down_dx")
    g["w_ff_down"], = _matmul(s["act"], dx3b, "tn", [F32], name="mm_down_dw")
    dh2, = _matmul(du, w["w_ff_up"], "nt", [F32], name="mm_up_dx")
    g["w_ff_up"], = _matmul(s["h2"], du, "tn", [F32], name="mm_up_dw")
    dx2, g["norm2_g"] = _rms_bwd(dh2, s["x2"], w["norm2_g"], dx3, "rms2_bwd")
    dx2b = dx2.astype(BF16)
    dmix, = _matmul(dx2b, w["w_out"], "nt", [F32], name="mm_out_dx")
    g["w_out"], = _matmul(s["mix"], dx2b, "tn", [F32], name="mm_out_dw")
    dy_gdn, dy_swa, dgg, dgs = _gate_mix_bwd(dmix, s["proj"], s["y_gdn"], s["y_swa"])
    do_gdn, = _matmul(dy_gdn, w["w_branch_gdn"], "nt", [BF16], name="mm_bgdn_dx")
    g["w_branch_gdn"], = _matmul(s["o_gdn"], dy_gdn, "tn", [F32], name="mm_bgdn_dw")
    do_swa, = _matmul(dy_swa, w["w_branch_swa"], "nt", [BF16], name="mm_bswa_dx")
    g["w_branch_swa"], = _matmul(s["o_swa"], dy_swa, "tn", [F32], name="mm_bswa_dw")
    a_rows, b_rows, q_s, k_s, v_s = _branch_inputs(s["proj"])
    dq_s, dk_s, dv_s, dsink = _swa_bwd(q_s, k_s, v_s, w["attn_sinks"].reshape(-1, 1, 1), _to_heads(do_swa, SWA_HEADS))
    g["attn_sinks"] = dsink.reshape(-1)
    dy, dz, da_rows, db_rows, dalog, ddt, dng = _gdn_bwd(
        s["y"], s["proj"], a_rows, b_rows, w["a_log"].reshape(-1, 1, 1), w["dt_bias"].reshape(-1, 1, 1),
        w["gdn_norm_g"].reshape(1, 1, -1), s["states"], do_gdn)
    g["a_log"], g["dt_bias"], g["gdn_norm_g"] = dalog.reshape(-1), ddt.reshape(-1), dng.reshape(-1)
    dqkv, g["conv_w"] = _conv_bwd(dy, s["proj"], w["conv_w"])
    dba = jnp.concatenate([_unrows(db_rows), _unrows(da_rows)], axis=1).astype(BF16)
    dproj = jnp.concatenate(
        [dqkv, dz, dgg, dgs, _from_heads(dq_s), _from_heads(dk_s), _from_heads(dv_s), dba,
         jnp.zeros((T, N_PROJ - BA0 - 2 * GDN_HEADS), BF16)], axis=1)
    dh1, = _matmul(dproj, w["w_in"], "nt", [F32], name="mm_proj_dx")
    g["w_in"], = _matmul(s["h1"], dproj, "tn", [F32], name="mm_proj_dw")
    dx, g["norm1_g"] = _rms_bwd(dh1, s["x"], w["norm1_g"], dx2, "rms1_bwd")
    g["norm1_g"], g["norm2_g"] = g["norm1_g"].reshape(-1), g["norm2_g"].reshape(-1)
    return dx, g


_LAYER_KEYS = ("norm1_g", "w_in", "conv_w", "a_log", "dt_bias", "gdn_norm_g", "attn_sinks", "w_branch_gdn",
               "w_branch_swa", "w_out", "norm2_g", "w_ff_up", "w_ff_down")


def _local_step(x, tgt, weights, final_norm_g):
    saved = []
    for wl in weights:
        x, s = _layer_fwd(x, wl)
        saved.append(s)
    loss, dx, dgf = _loss_head(x, final_norm_g, tgt)
    grads = [None] * len(weights)
    for l in reversed(range(len(weights))):
        dx, grads[l] = _layer_bwd(dx, weights[l], saved[l])
    return loss, dx, grads, dgf


def _my_index():
    return 4 * lax.axis_index("x") + 2 * lax.axis_index("y") + lax.axis_index("c")


def _peer(mask):
    x, y, c = lax.axis_index("x"), lax.axis_index("y"), lax.axis_index("c")
    px = 1 - x if mask & 4 else x
    py = 1 - y if mask & 2 else y
    pc = 1 - c if mask & 1 else c
    return (px, py, pc), 4 * px + 2 * py + pc


def _all_gather(shards):
    n = len(shards)

    def body(*refs):
        ins, outs = refs[:n], refs[n:2 * n]
        send_sems, recv_sems, local_sems = refs[2 * n:]
        me = _my_index()
        sibling, sib_idx = _peer(1)
        chips = [_peer(m) for m in (4, 2, 6)]

        def copy(a, k, block, to, src=None):
            return pltpu.make_async_remote_copy(
                src_ref=outs[a].at[block] if src is None else src, dst_ref=outs[a].at[block],
                send_sem=send_sems.at[a, k], recv_sem=recv_sems.at[a, k], device_id=to, device_id_type=MESH)

        mine = [pltpu.make_async_copy(ins[a], outs[a].at[me], local_sems.at[a]) for a in range(n)]
        for cp in mine:
            cp.start()
        first = []
        for a in range(n):
            first.append(copy(a, 0, me, sibling, src=ins[a]))
            first += [copy(a, 1 + j, me, dev, src=ins[a]) for j, (dev, _) in enumerate(chips)]
        for cp in first:
            cp.start()
        passed = []
        for a in range(n):
            for j, (_, idx) in enumerate(chips):
                copy(a, 1 + j, idx, sibling).wait_recv()
                cp = copy(a, 4 + j, idx, sibling)
                cp.start()
                passed.append(cp)
        for a in range(n):
            copy(a, 0, sib_idx, sibling).wait_recv()
            for j, (_, idx) in enumerate(chips):
                copy(a, 4 + j, idx ^ 1, sibling).wait_recv()
        for cp in first + passed:
            cp.wait_send()
        for cp in mine:
            cp.wait()

    any_spec = pl.BlockSpec(memory_space=pl.ANY)
    return pl.pallas_call(
        body, in_specs=[any_spec] * n, out_specs=[any_spec] * n,
        out_shape=[jax.ShapeDtypeStruct((N_DEV,) + s.shape, s.dtype) for s in shards],
        scratch_shapes=[pltpu.SemaphoreType.DMA((n, 7)), pltpu.SemaphoreType.DMA((n, 7)), pltpu.SemaphoreType.DMA((n,))],
        name="weight_all_gather",
    )(*shards)


def _scatter(blocked):
    n = len(blocked)

    def body(*refs):
        ins, outs = refs[:n], refs[n:2 * n]
        send_sems, recv_sems, local_sems = refs[2 * n:]
        me = _my_index()
        mine = [pltpu.make_async_copy(ins[a].at[me], outs[a].at[me], local_sems.at[a]) for a in range(n)]
        for cp in mine:
            cp.start()
        sends = []
        for a in range(n):
            for mask in range(1, N_DEV):
                dev, idx = _peer(mask)
                sends.append(pltpu.make_async_remote_copy(
                    src_ref=ins[a].at[idx], dst_ref=outs[a].at[me], send_sem=send_sems.at[a, mask - 1],
                    recv_sem=recv_sems.at[a, mask - 1], device_id=dev, device_id_type=MESH))
        for cp in sends:
            cp.start()
        for a in range(n):
            for mask in range(1, N_DEV):
                dev, idx = _peer(mask)
                pltpu.make_async_remote_copy(
                    src_ref=ins[a].at[idx], dst_ref=outs[a].at[idx], send_sem=send_sems.at[a, mask - 1],
                    recv_sem=recv_sems.at[a, mask - 1], device_id=dev, device_id_type=MESH).wait_recv()
        for cp in sends:
            cp.wait_send()
        for cp in mine:
            cp.wait()

    any_spec = pl.BlockSpec(memory_space=pl.ANY)
    return pl.pallas_call(
        body, in_specs=[any_spec] * n, out_specs=[any_spec] * n,
        out_shape=[jax.ShapeDtypeStruct(b.shape, b.dtype) for b in blocked],
        scratch_shapes=[pltpu.SemaphoreType.DMA((n, 7)), pltpu.SemaphoreType.DMA((n, 7)), pltpu.SemaphoreType.DMA((n,))],
        name="grad_scatter",
    )(*blocked)


def _all_reduce_small(part):
    R = part.shape[0]

    def body(x_ref, sum_ref, gath, send_sems, recv_sems):
        me = _my_index()
        gath[me] = x_ref[...]
        sends = []
        for mask in range(1, N_DEV):
            dev, _ = _peer(mask)
            sends.append(pltpu.make_async_remote_copy(
                src_ref=x_ref, dst_ref=gath.at[me], send_sem=send_sems.at[mask - 1], recv_sem=recv_sems.at[mask - 1],
                device_id=dev, device_id_type=MESH))
        for cp in sends:
            cp.start()
        for mask in range(1, N_DEV):
            dev, idx = _peer(mask)
            pltpu.make_async_remote_copy(
                src_ref=x_ref, dst_ref=gath.at[idx], send_sem=send_sems.at[mask - 1], recv_sem=recv_sems.at[mask - 1],
                device_id=dev, device_id_type=MESH).wait_recv()
        for cp in sends:
            cp.wait_send()
        acc = gath[0]
        for i in range(1, N_DEV):
            acc = acc + gath[i]
        sum_ref[...] = acc

    vm = pl.BlockSpec(memory_space=pltpu.VMEM)
    return pl.pallas_call(
        body, in_specs=[vm], out_specs=vm, out_shape=jax.ShapeDtypeStruct((R, 128), F32),
        scratch_shapes=[pltpu.VMEM((N_DEV, R, 128), F32), pltpu.SemaphoreType.DMA((7,)), pltpu.SemaphoreType.DMA((7,))],
        name="small_all_reduce",
    )(part)


def _adam_math(w, g, m, v):
    m = ADAM_B1 * m + (1.0 - ADAM_B1) * g
    v = ADAM_B2 * v + (1.0 - ADAM_B2) * (g * g)
    m_hat = m / (1.0 - ADAM_B1 ** ADAM_STEP)
    v_hat = v / (1.0 - ADAM_B2 ** ADAM_STEP)
    delta = -ADAM_LR * (m_hat / (jnp.sqrt(v_hat) + ADAM_EPS) + ADAM_WD * w)
    return delta, m, v


def _adamw_sum(parts, w, m, v, name):
    n_layers = len(parts)
    R, Cc = parts[0].shape[1:]
    tr = R
    while tr * Cc * 4 > (1 << 20) and tr % 32 == 0:
        tr //= 2
    nblk = R // tr

    def body(*refs):
        p_refs = refs[:n_layers]
        w_ref, m_ref, v_ref, g_ref, d_ref, nm_ref, nv_ref = refs[n_layers:]
        for j in range(n_layers):
            @pl.when(pl.program_id(0) == j)
            def _(p_ref=p_refs[j]):
                g = p_ref[0].astype(F32)
                for i in range(1, N_DEV):
                    g = g + p_ref[i].astype(F32)
                d, nm, nv = _adam_math(w_ref[...], g, m_ref[...], v_ref[...])
                g_ref[...], d_ref[...], nm_ref[...], nv_ref[...] = g, d, nm, nv

    def part_spec(j):
        return pl.BlockSpec((N_DEV, tr, Cc), lambda l, i: (0, jnp.where(l == j, i, jnp.where(l < j, 0, nblk - 1)), 0))

    blk = pl.BlockSpec((tr, Cc), lambda l, i: (l * nblk + i, 0))
    out = jax.ShapeDtypeStruct((n_layers * R, Cc), F32)
    return pl.pallas_call(
        body, grid=(n_layers, nblk), in_specs=[part_spec(j) for j in range(n_layers)] + [blk, blk, blk],
        out_specs=[blk] * 4, out_shape=[out] * 4, compiler_params=_cp("arbitrary", "arbitrary"), name=name,
    )(*parts, w, m, v)


def _adamw_small(g, w, m, v):
    def body(g_ref, w_ref, m_ref, v_ref, d_ref, nm_ref, nv_ref):
        d_ref[...], nm_ref[...], nv_ref[...] = _adam_math(w_ref[...], g_ref[...], m_ref[...], v_ref[...])

    out = jax.ShapeDtypeStruct(g.shape, F32)
    return pl.pallas_call(body, out_shape=[out] * 3, name="adamw_small")(g, w, m, v)


def _align_w_in(w_full):
    by_dst = sorted(_SEGS, key=lambda s: s[2])
    parts = [w_full[:, src:src + width] for src, width, _ in by_dst]
    end = by_dst[-1][2] + by_dst[-1][1]
    return jnp.concatenate(parts + [jnp.zeros((w_full.shape[0], N_PROJ - end), w_full.dtype)], axis=1)


def _unalign_w_in(g):
    return jnp.concatenate([g[:, dst:dst + width] for _, width, dst in _SEGS], axis=1)


_SHARDED = {"w_in": 1, "conv_w": 1, "w_branch_gdn": 0, "w_branch_swa": 0, "w_out": 0, "w_ff_up": 1, "w_ff_down": 0}
_SMALL = ("norm1_g", "a_log", "dt_bias", "gdn_norm_g", "attn_sinks", "norm2_g", "final_norm_g")


def _unblock(gathered, axis):
    if axis == 0:
        return gathered.reshape(-1, gathered.shape[2])
    return gathered.transpose(1, 0, 2).reshape(gathered.shape[1], -1)


def _block(full, axis):
    A, B = full.shape
    if axis == 0:
        return full.reshape(N_DEV, A // N_DEV, B)
    return full.reshape(A, N_DEV, B // N_DEV).transpose(1, 0, 2)


def _pack_small(d, loss_row=None):
    rows = [d[k].astype(F32).reshape(-1, 128) if d[k].size % 128 == 0 else
            jnp.pad(d[k].astype(F32), ((0, 0), (0, 128 - d[k].shape[-1]))) for k in _SMALL]
    rows.append(jnp.zeros((1, 128), F32) if loss_row is None else loss_row)
    packed = jnp.concatenate(rows, axis=0)
    return jnp.pad(packed, ((0, -packed.shape[0] % 8), (0, 0)))


def _unpack_small(packed, like):
    out, r = {}, 0
    for k in _SMALL:
        shp = like[k].shape
        if like[k].size % 128 == 0:
            n = like[k].size // 128
            out[k] = packed[r:r + n].reshape(shp)
        else:
            n = shp[0]
            out[k] = packed[r:r + n, :shp[-1]]
        r += n
    return out, packed[r, 0]


def kernel(x, norm1_g, w_in, conv_w, a_log, dt_bias, gdn_norm_g, attn_sinks, w_branch_gdn, w_branch_swa, w_out, norm2_g, w_ff_up, w_ff_down, final_norm_g, loss_target, m_norm1_g, m_w_in, m_conv_w, m_a_log, m_dt_bias, m_gdn_norm_g, m_attn_sinks, m_w_branch_gdn, m_w_branch_swa, m_w_out, m_norm2_g, m_w_ff_up, m_w_ff_down, m_final_norm_g, v_norm1_g, v_w_in, v_conv_w, v_a_log, v_dt_bias, v_gdn_norm_g, v_attn_sinks, v_w_branch_gdn, v_w_branch_swa, v_w_out, v_norm2_g, v_w_ff_up, v_w_ff_down, v_final_norm_g):
    names = ("norm1_g", "w_in", "conv_w", "a_log", "dt_bias", "gdn_norm_g", "attn_sinks", "w_branch_gdn",
             "w_branch_swa", "w_out", "norm2_g", "w_ff_up", "w_ff_down", "final_norm_g")
    w = dict(zip(names, (norm1_g, w_in, conv_w, a_log, dt_bias, gdn_norm_g, attn_sinks, w_branch_gdn, w_branch_swa,
                         w_out, norm2_g, w_ff_up, w_ff_down, final_norm_g)))
    m = dict(zip(names, (m_norm1_g, m_w_in, m_conv_w, m_a_log, m_dt_bias, m_gdn_norm_g, m_attn_sinks, m_w_branch_gdn,
                         m_w_branch_swa, m_w_out, m_norm2_g, m_w_ff_up, m_w_ff_down, m_final_norm_g)))
    v = dict(zip(names, (v_norm1_g, v_w_in, v_conv_w, v_a_log, v_dt_bias, v_gdn_norm_g, v_attn_sinks, v_w_branch_gdn,
                         v_w_branch_swa, v_w_out, v_norm2_g, v_w_ff_up, v_w_ff_down, v_final_norm_g)))
    sharded = list(_SHARDED)

    layer_w = []
    for l in range(DEPTH):
        gathered = _all_gather([w[k][l] if k == "conv_w" else w[k][l].astype(BF16) for k in sharded])
        wl = {k: _unblock(gth, _SHARDED[k]) for k, gth in zip(sharded, gathered)}
        wl["w_in"] = _align_w_in(wl["w_in"])
        wl.update({k: w[k][l] for k in _LAYER_KEYS if k not in wl})
        layer_w.append(wl)

    loss_row, grad_x, grads, dgf = _local_step(x[0], loss_target[0], layer_w, final_norm_g)

    landed = []
    for gl in grads:
        gl["w_in"] = _unalign_w_in(gl["w_in"])
        landed.append(_scatter([_block(gl[k], _SHARDED[k]).astype(BF16) for k in sharded]))
    out_g, out_d, out_m, out_v = {}, {}, {}, {}
    for i, k in enumerate(sharded):
        shp = w[k].shape
        flat = lambda t: t.reshape(-1, shp[-1])
        parts = [landed[l][i] for l in range(DEPTH)]
        if parts[0].shape[1] % 8:
            parts = [jnp.concatenate(parts, axis=1)]
        res = _adamw_sum(parts, flat(w[k]), flat(m[k]), flat(v[k]), "adamw_" + k)
        out_g[k], out_d[k], out_m[k], out_v[k] = [t.reshape(shp) for t in res]

    small_g = {k: jnp.stack([gl[k] for gl in grads]) for k in _SMALL if k != "final_norm_g"}
    small_g["final_norm_g"] = dgf.reshape(-1)
    total = _all_reduce_small(_pack_small(small_g, loss_row))
    sd, sm, sv = _adamw_small(total, _pack_small(w), _pack_small(m), _pack_small(v))
    g_small, loss = _unpack_small(total, w)
    d_small, _ = _unpack_small(sd, w)
    m_small, _ = _unpack_small(sm, w)
    v_small, _ = _unpack_small(sv, w)
    out_g.update(g_small), out_d.update(d_small), out_m.update(m_small), out_v.update(v_small)

    return (loss, grad_x[None], *[out_g[k] for k in names], *[out_d[k] for k in names],
            *[out_m[k] for k in names], *[out_v[k] for k in names])
```

```python
import functools
import math

import jax
import jax.numpy as jnp
from jax import lax
from jax.experimental import pallas as pl
from jax.experimental.pallas import tpu as pltpu

F32 = jnp.float32
BF16 = jnp.bfloat16
HI = lax.Precision.HIGHEST
MESH = pl.DeviceIdType.MESH

N_DEV = 8
D_MODEL = 2048
DEPTH = 4
GDN_HEADS = 16
HEAD_DIM = 128
CHUNK = 64
CONV_K = 4
SWA_HEADS = 32
SWA_KV = 4
SWA_GROUP = SWA_HEADS // SWA_KV
SWA_DIM = 64
WINDOW = 128
D_FF = 4 * D_MODEL
D_IN = 14880
NORM_EPS = 1e-6
GDN_W = GDN_HEADS * HEAD_DIM
KV_W = SWA_KV * SWA_DIM

QKV0, Z0, GG0, GS0, QS0, KS0, VS0, BA0 = 0, 6144, 8192, 10240, 12288, 14336, 14592, 14848
N_PROJ = 15360
_SEGS = ((0, 6144, QKV0), (6144, 2048, Z0), (8192, 32, BA0), (8224, 2048, QS0), (10272, 256, KS0),
         (10528, 256, VS0), (10784, 4096, GG0))

ADAM_LR, ADAM_B1, ADAM_B2, ADAM_EPS, ADAM_WD, ADAM_STEP = 0.001, 0.9, 0.999, 1e-08, 0.01, 10

VMEM_LIMIT = 56 * 1024 * 1024
GDN_HB = 4
GDN_PREC_SOLVE = "x3"
GDN_PREC = "bf16"
SWA_PREC = "bf16"


def _cp(*sem):
    return pltpu.CompilerParams(dimension_semantics=sem, vmem_limit_bytes=VMEM_LIMIT)


def _tile(n, want):
    t = min(n, want)
    while n % t:
        t -= 128
    return t


def _my_index():
    return 4 * lax.axis_index("x") + 2 * lax.axis_index("y") + lax.axis_index("c")


def _peer(mask):
    x, y, c = lax.axis_index("x"), lax.axis_index("y"), lax.axis_index("c")
    px = 1 - x if mask & 4 else x
    py = 1 - y if mask & 2 else y
    pc = 1 - c if mask & 1 else c
    return (px, py, pc), 4 * px + 2 * py + pc


class _Exchange:
    def __init__(self, kind, arrays):
        self.kind, self.arrays, self.n = kind, list(arrays), len(arrays)
        self.out_shape = [jax.ShapeDtypeStruct(((N_DEV,) + a.shape) if kind == "gather" else a.shape, a.dtype)
                          for a in self.arrays]
        self.scratch = [pltpu.SemaphoreType.DMA((self.n, 7)), pltpu.SemaphoreType.DMA((self.n, 7)),
                        pltpu.SemaphoreType.DMA((self.n,))]

    def _local(self, ins, outs, sems):
        me = _my_index()
        src = (lambda a: ins[a]) if self.kind == "gather" else (lambda a: ins[a].at[me])
        return [pltpu.make_async_copy(src(a), outs[a].at[me], sems[2].at[a]) for a in range(self.n)]

    def _copy(self, outs, sems, a, k, src, block, to):
        return pltpu.make_async_remote_copy(src_ref=src, dst_ref=outs[a].at[block], send_sem=sems[0].at[a, k],
                                            recv_sem=sems[1].at[a, k], device_id=to, device_id_type=MESH)

    def _sends(self, ins, outs, sems):
        me = _my_index()
        cps = []
        for a in range(self.n):
            if self.kind == "gather":
                for k, mask in enumerate((1, 4, 2, 6)):
                    cps.append(self._copy(outs, sems, a, k, ins[a], me, _peer(mask)[0]))
            else:
                for mask in range(1, N_DEV):
                    dev, idx = _peer(mask)
                    cps.append(self._copy(outs, sems, a, mask - 1, ins[a].at[idx], me, dev))
        return cps

    def _relays(self, outs, sems):
        sibling = _peer(1)[0]
        return [self._copy(outs, sems, a, 4 + j, outs[a].at[_peer(mask)[1]], _peer(mask)[1], sibling)
                for a in range(self.n) for j, mask in enumerate((4, 2, 6))]

    def start(self, ins, outs, sems):
        for cp in self._local(ins, outs, sems) + self._sends(ins, outs, sems):
            cp.start()

    def finish(self, ins, outs, sems):
        me = _my_index()
        if self.kind == "gather":
            relays = self._relays(outs, sems)
            for a in range(self.n):
                for j, mask in enumerate((4, 2, 6)):
                    self._copy(outs, sems, a, 1 + j, ins[a], _peer(mask)[1], _peer(mask)[0]).wait_recv()
                    relays[3 * a + j].start()
            for a in range(self.n):
                self._copy(outs, sems, a, 0, ins[a], _peer(1)[1], _peer(1)[0]).wait_recv()
                for j, mask in enumerate((4, 2, 6)):
                    self._copy(outs, sems, a, 4 + j, ins[a], _peer(mask)[1] ^ 1, _peer(1)[0]).wait_recv()
            for cp in relays:
                cp.wait_send()
        else:
            for a in range(self.n):
                for mask in range(1, N_DEV):
                    dev, idx = _peer(mask)
                    self._copy(outs, sems, a, mask - 1, ins[a].at[idx], idx, dev).wait_recv()
        for cp in self._sends(ins, outs, sems):
            cp.wait_send()
        for cp in self._local(ins, outs, sems):
            cp.wait()


def _exchange(kind, arrays, name):
    ex = _Exchange(kind, arrays)

    def body(*refs):
        ins, outs, sems = refs[:ex.n], refs[ex.n:2 * ex.n], refs[2 * ex.n:]
        ex.start(ins, outs, sems)
        ex.finish(ins, outs, sems)

    any_spec = pl.BlockSpec(memory_space=pl.ANY)
    return pl.pallas_call(body, in_specs=[any_spec] * ex.n, out_specs=[any_spec] * ex.n, out_shape=ex.out_shape,
                          scratch_shapes=ex.scratch, name=name)(*ex.arrays)


def _call(body, args, *, grid, in_specs, out_specs, out_shape, scratch_shapes=(), sem, name, carried=None):
    if carried is None:
        outs = pl.pallas_call(body, grid=grid, in_specs=list(in_specs), out_specs=list(out_specs),
                              out_shape=list(out_shape), scratch_shapes=list(scratch_shapes),
                              compiler_params=_cp(*sem), name=name)(*args)
        return outs, None
    n_in, n_out, n_scr, n_c = len(args), len(out_shape), len(scratch_shapes), carried.n

    def wrapped(*refs):
        ins, c_in = refs[:n_in], refs[n_in:n_in + n_c]
        o0 = n_in + n_c
        outs, c_out = refs[o0:o0 + n_out], refs[o0 + n_out:o0 + n_out + n_c]
        s0 = o0 + n_out + n_c
        scr, c_sems = refs[s0:s0 + n_scr], refs[s0 + n_scr:]
        ids = [pl.program_id(i) for i in range(len(grid))]
        first = functools.reduce(jnp.logical_and, [i == 0 for i in ids])
        last = functools.reduce(jnp.logical_and, [i == g - 1 for i, g in zip(ids, grid)])

        @pl.when(first)
        def _():
            carried.start(c_in, c_out, c_sems)

        body(*ins, *outs, *scr)

        @pl.when(last)
        def _():
            carried.finish(c_in, c_out, c_sems)

    any_spec = pl.BlockSpec(memory_space=pl.ANY)
    res = pl.pallas_call(
        wrapped, grid=grid, in_specs=list(in_specs) + [any_spec] * n_c, out_specs=list(out_specs) + [any_spec] * n_c,
        out_shape=list(out_shape) + carried.out_shape, scratch_shapes=list(scratch_shapes) + carried.scratch,
        compiler_params=_cp(*["arbitrary"] * len(grid)), name=name)(*args, *carried.arrays)
    return res[:n_out], res[n_out:]


def _matmul(a, b, mode, out_shapes, epilogue=None, extras=(), name="mm", carried=None):
    if mode == "tn":
        K, M = a.shape
    else:
        M, K = a.shape
    N = b.shape[0] if mode == "nt" else b.shape[1]
    tm, tn, tk = _tile(M, 1024), _tile(N, 1024), _tile(K, 512)
    nk = K // tk
    dims = {"nn": (((1,), (0,)), ((), ())), "nt": (((1,), (1,)), ((), ())), "tn": (((0,), (0,)), ((), ()))}[mode]
    n_ex, n_out = len(extras), len(out_shapes)

    def body(*refs):
        a_ref, b_ref = refs[:2]
        ex = refs[2:2 + n_ex]
        outs = refs[2 + n_ex:2 + n_ex + n_out]
        acc = refs[-1]
        k = pl.program_id(2)

        @pl.when(k == 0)
        def _():
            acc[...] = jnp.zeros_like(acc)

        acc[...] += lax.dot_general(a_ref[...].astype(BF16), b_ref[...].astype(BF16), dims,
                                    preferred_element_type=F32)

        @pl.when(k == nk - 1)
        def _():
            res = (acc[...],) if epilogue is None else epilogue(acc[...], *[e[...] for e in ex])
            for o, r in zip(outs, res):
                o[...] = r.astype(o.dtype)

    a_spec = pl.BlockSpec((tk, tm), lambda i, j, k: (k, i)) if mode == "tn" else pl.BlockSpec((tm, tk), lambda i, j, k: (i, k))
    b_spec = pl.BlockSpec((tn, tk), lambda i, j, k: (j, k)) if mode == "nt" else pl.BlockSpec((tk, tn), lambda i, j, k: (k, j))
    mn = pl.BlockSpec((tm, tn), lambda i, j, k: (i, j))
    outs, moved = _call(
        body, (a, b, *extras), grid=(M // tm, N // tn, nk),
        in_specs=[a_spec, b_spec] + [mn] * n_ex, out_specs=[mn] * n_out,
        out_shape=[jax.ShapeDtypeStruct((M, N), dt) for dt in out_shapes],
        scratch_shapes=[pltpu.VMEM((tm, tn), F32)],
        sem=("parallel", "parallel", "arbitrary"), name=name, carried=carried)
    return list(outs) if carried is None else list(outs) + [moved]


def _rms_fwd(x, g, name):
    T, D = x.shape
    tr = _tile(T, 256)

    def body(x_ref, g_ref, h_ref):
        xv = x_ref[...]
        r = lax.rsqrt(jnp.mean(xv * xv, axis=-1, keepdims=True) + NORM_EPS)
        h_ref[...] = (xv * r * g_ref[...]).astype(BF16)

    return pl.pallas_call(
        body, grid=(T // tr,),
        in_specs=[pl.BlockSpec((tr, D), lambda i: (i, 0)), pl.BlockSpec((1, D), lambda i: (0, 0))],
        out_specs=pl.BlockSpec((tr, D), lambda i: (i, 0)),
        out_shape=jax.ShapeDtypeStruct((T, D), BF16), compiler_params=_cp("parallel"), name=name,
    )(x, g.reshape(1, D))


def _rms_bwd(dh, x, g, dres, name):
    T, D = x.shape
    tr = _tile(T, 256)

    def body(dh_ref, x_ref, g_ref, dres_ref, dx_ref, dg_ref):
        @pl.when(pl.program_id(0) == 0)
        def _():
            dg_ref[...] = jnp.zeros_like(dg_ref)

        xv = x_ref[...]
        r = lax.rsqrt(jnp.mean(xv * xv, axis=-1, keepdims=True) + NORM_EPS)
        xhat = xv * r
        dhv = dh_ref[...].astype(F32)
        gd = dhv * g_ref[...]
        dx_ref[...] = dres_ref[...] + r * (gd - xhat * jnp.mean(gd * xhat, axis=-1, keepdims=True))
        dg_ref[...] += jnp.sum(dhv * xhat, axis=0, keepdims=True)

    row = pl.BlockSpec((tr, D), lambda i: (i, 0))
    vec = pl.BlockSpec((1, D), lambda i: (0, 0))
    return pl.pallas_call(
        body, grid=(T // tr,), in_specs=[row, row, vec, row], out_specs=[row, vec],
        out_shape=[jax.ShapeDtypeStruct((T, D), F32), jax.ShapeDtypeStruct((1, D), F32)],
        compiler_params=_cp("arbitrary"), name=name,
    )(dh, x, g.reshape(1, D), dres)


def _loss_head(x, g, tgt):
    T, D = x.shape
    tr = _tile(T, 256)

    def body(x_ref, g_ref, t_ref, loss_ref, dx_ref, dg_ref):
        @pl.when(pl.program_id(0) == 0)
        def _():
            dg_ref[...] = jnp.zeros_like(dg_ref)
            loss_ref[...] = jnp.zeros_like(loss_ref)

        xv = x_ref[...]
        r = lax.rsqrt(jnp.mean(xv * xv, axis=-1, keepdims=True) + NORM_EPS)
        xhat = xv * r
        err = xhat * g_ref[...] - t_ref[...]
        loss_ref[...] += (0.5 / D) * jnp.sum(jnp.sum(err * err, axis=-1, keepdims=True), axis=0, keepdims=True)
        dy = err * (1.0 / D)
        gd = dy * g_ref[...]
        dx_ref[...] = r * (gd - xhat * jnp.mean(gd * xhat, axis=-1, keepdims=True))
        dg_ref[...] += jnp.sum(dy * xhat, axis=0, keepdims=True)

    row = pl.BlockSpec((tr, D), lambda i: (i, 0))
    vec = pl.BlockSpec((1, D), lambda i: (0, 0))
    return pl.pallas_call(
        body, grid=(T // tr,), in_specs=[row, vec, row],
        out_specs=[pl.BlockSpec((1, 128), lambda i: (0, 0)), row, vec],
        out_shape=[jax.ShapeDtypeStruct((1, 128), F32), jax.ShapeDtypeStruct((T, D), F32),
                   jax.ShapeDtypeStruct((1, D), F32)],
        compiler_params=_cp("arbitrary"), name="loss_head",
    )(x, g.reshape(1, D), tgt)


def _sigmoid(x):
    return 1.0 / (1.0 + jnp.exp(-x))


def _gate_mix_fwd(proj, y_gdn, y_swa):
    T = proj.shape[0]
    tr, tc = _tile(T, 512), 512
    nc = D_MODEL // tc

    def body(gg_ref, gs_ref, yg_ref, ys_ref, mix_ref):
        mix_ref[...] = (_sigmoid(gg_ref[...]) * yg_ref[...] + _sigmoid(gs_ref[...]) * ys_ref[...]).astype(BF16)

    blk = pl.BlockSpec((tr, tc), lambda i, j: (i, j))
    return pl.pallas_call(
        body, grid=(T // tr, nc),
        in_specs=[pl.BlockSpec((tr, tc), lambda i, j: (i, GG0 // tc + j)),
                  pl.BlockSpec((tr, tc), lambda i, j: (i, GS0 // tc + j)), blk, blk],
        out_specs=blk, out_shape=jax.ShapeDtypeStruct((T, D_MODEL), BF16),
        compiler_params=_cp("parallel", "parallel"), name="gate_mix_fwd",
    )(proj, proj, y_gdn, y_swa)


def _gate_mix_bwd(dmix, proj, y_gdn, y_swa):
    T = proj.shape[0]
    tr, tc = _tile(T, 512), 512
    nc = D_MODEL // tc

    def body(dm_ref, gg_ref, gs_ref, yg_ref, ys_ref, dyg_ref, dys_ref, dgg_ref, dgs_ref):
        dm = dm_ref[...]
        sg, ss = _sigmoid(gg_ref[...]), _sigmoid(gs_ref[...])
        dyg_ref[...] = (dm * sg).astype(BF16)
        dys_ref[...] = (dm * ss).astype(BF16)
        dgg_ref[...] = (dm * yg_ref[...] * sg * (1.0 - sg)).astype(BF16)
        dgs_ref[...] = (dm * ys_ref[...] * ss * (1.0 - ss)).astype(BF16)

    blk = pl.BlockSpec((tr, tc), lambda i, j: (i, j))
    out = jax.ShapeDtypeStruct((T, D_MODEL), BF16)
    return pl.pallas_call(
        body, grid=(T // tr, nc),
        in_specs=[blk, pl.BlockSpec((tr, tc), lambda i, j: (i, GG0 // tc + j)),
                  pl.BlockSpec((tr, tc), lambda i, j: (i, GS0 // tc + j)), blk, blk],
        out_specs=[blk] * 4, out_shape=[out] * 4,
        compiler_params=_cp("parallel", "parallel"), name="gate_mix_bwd",
    )(dmix, proj, proj, y_gdn, y_swa)


CONV_TT, CONV_TW, HALO = 512, 256, 8


def _conv_fwd(proj, conv_w):
    T = proj.shape[0]
    tt = _tile(T, CONV_TT)
    W = 3 * GDN_W

    def body(x_ref, halo_ref, w_ref, y_ref):
        first = pl.program_id(1) == 0
        halo = jnp.where(first, 0.0, halo_ref[...])
        xe = jnp.concatenate([halo, x_ref[...]], axis=0)
        acc = xe[HALO:] * w_ref[pl.ds(CONV_K - 1, 1), :]
        for j in range(CONV_K - 1):
            acc = acc + pltpu.roll(xe, CONV_K - 1 - j, 0)[HALO:] * w_ref[pl.ds(j, 1), :]
        y_ref[...] = acc

    return pl.pallas_call(
        body, grid=(W // CONV_TW, T // tt),
        in_specs=[pl.BlockSpec((tt, CONV_TW), lambda c, t: (t, c)),
                  pl.BlockSpec((HALO, CONV_TW), lambda c, t: (jnp.maximum(t * (tt // HALO) - 1, 0), c)),
                  pl.BlockSpec((CONV_K, CONV_TW), lambda c, t: (0, c))],
        out_specs=pl.BlockSpec((tt, CONV_TW), lambda c, t: (t, c)),
        out_shape=jax.ShapeDtypeStruct((T, W), F32),
        compiler_params=_cp("parallel", "parallel"), name="conv_fwd",
    )(proj, proj, conv_w)


def _conv_bwd(dy, proj, conv_w):
    T = proj.shape[0]
    tt = _tile(T, CONV_TT)
    nt = T // tt
    W = 3 * GDN_W

    def body(dy_ref, dnext_ref, x_ref, halo_ref, w_ref, dx_ref, dw_ref):
        t = pl.program_id(1)

        @pl.when(t == 0)
        def _():
            dw_ref[...] = jnp.zeros_like(dw_ref)

        dyv = dy_ref[...]
        dye = jnp.concatenate([dyv, jnp.where(t == nt - 1, 0.0, dnext_ref[...])], axis=0)
        xe = jnp.concatenate([jnp.where(t == 0, 0.0, halo_ref[...]), x_ref[...]], axis=0)
        acc = dyv * w_ref[pl.ds(CONV_K - 1, 1), :]
        dw_ref[pl.ds(CONV_K - 1, 1), :] += jnp.sum(dyv * xe[HALO:], axis=0, keepdims=True)
        for s in range(1, CONV_K):
            j = CONV_K - 1 - s
            acc = acc + pltpu.roll(dye, tt + HALO - s, 0)[:tt] * w_ref[pl.ds(j, 1), :]
            dw_ref[pl.ds(j, 1), :] += jnp.sum(dyv * pltpu.roll(xe, s, 0)[HALO:], axis=0, keepdims=True)
        dx_ref[...] = acc.astype(BF16)

    cur = pl.BlockSpec((tt, CONV_TW), lambda c, t: (t, c))
    return pl.pallas_call(
        body, grid=(W // CONV_TW, nt),
        in_specs=[cur,
                  pl.BlockSpec((HALO, CONV_TW), lambda c, t: (jnp.minimum((t + 1) * (tt // HALO), T // HALO - 1), c)),
                  cur,
                  pl.BlockSpec((HALO, CONV_TW), lambda c, t: (jnp.maximum(t * (tt // HALO) - 1, 0), c)),
                  pl.BlockSpec((CONV_K, CONV_TW), lambda c, t: (0, c))],
        out_specs=[cur, pl.BlockSpec((CONV_K, CONV_TW), lambda c, t: (0, c))],
        out_shape=[jax.ShapeDtypeStruct((T, W), BF16), jax.ShapeDtypeStruct((CONV_K, W), F32)],
        compiler_params=_cp("parallel", "arbitrary"), name="conv_bwd",
    )(dy, dy, proj, proj, conv_w)


def _dot(a, b, kind, prec):
    nb = a.ndim - 2
    batch = tuple(range(nb))
    ca = nb if kind == "tn" else nb + 1
    cb = nb + 1 if kind == "nt" else nb
    dims = (((ca,), (cb,)), (batch, batch))
    if prec == "bf16":
        return lax.dot_general(a.astype(BF16), b.astype(BF16), dims, preferred_element_type=F32)
    return lax.dot_general(a, b, dims, precision=HI if prec == "f32" else lax.Precision.HIGH,
                           preferred_element_type=F32)


@functools.lru_cache(maxsize=None)
def _mm(kind, prec):
    @jax.custom_vjp
    def f(a, b):
        return _dot(a, b, kind, prec)

    def fwd(a, b):
        return f(a, b), (a, b)

    def bwd(res, ct):
        a, b = res
        if kind == "nn":
            return _dot(ct, b, "nt", prec), _dot(a, ct, "tn", prec)
        if kind == "nt":
            return _dot(ct, b, "nn", prec), _dot(ct, a, "tn", prec)
        return _dot(b, ct, "nt", prec), _dot(a, ct, "nn", prec)

    f.defvjp(fwd, bwd)
    return f


def _col_from_row(row, eye):
    return jnp.sum(jnp.where(eye, row, 0.0), axis=2, keepdims=True)


def _silu(x):
    return x / (1.0 + jnp.exp(-x))


def _gdn_chunk(yq, yk, yv, z, a_row, b_row, a_log, dt_bias, norm_g, state):
    h = yq.shape[0]
    ii = lax.broadcasted_iota(jnp.int32, (1, CHUNK, CHUNK), 1)
    jj = lax.broadcasted_iota(jnp.int32, (1, CHUNK, CHUNK), 2)
    eye = ii == jj
    qr, kr, v = _silu(yq), _silu(yk), _silu(yv)
    q = qr * lax.rsqrt(jnp.sum(qr * qr, axis=-1, keepdims=True) + NORM_EPS) * (HEAD_DIM ** -0.5)
    k = kr * lax.rsqrt(jnp.sum(kr * kr, axis=-1, keepdims=True) + NORM_EPS)
    beta_row = _sigmoid(b_row)
    xa = a_row + dt_bias
    g_row = -jnp.exp(a_log) * (jnp.maximum(xa, 0.0) + jnp.log(1.0 + jnp.exp(-jnp.abs(xa))))
    upper = jnp.broadcast_to(jnp.where(ii <= jj, 1.0, 0.0).astype(F32), (h, CHUNK, CHUNK))
    decay_row = _mm("nn", "f32")(g_row, upper)
    decay_col = _col_from_row(decay_row, eye)
    beta_col = _col_from_row(beta_row, eye)
    decay_last = jnp.sum(g_row, axis=2, keepdims=True)
    gamma = jnp.exp(jnp.where(ii >= jj, decay_col - decay_row, -jnp.inf))
    k_beta = k * beta_col
    a_low = jnp.where(ii > jj, _mm("nt", GDN_PREC_SOLVE)(k_beta, k) * gamma, 0.0)
    t_inv = jnp.where(eye, 1.0, 0.0).astype(F32) - a_low
    p = a_low
    for _ in range(5):
        p = _mm("nn", GDN_PREC_SOLVE)(p, p)
        t_inv = t_inv + _mm("nn", GDN_PREC_SOLVE)(t_inv, p)
    e_dec = jnp.exp(decay_col)
    u = _mm("nn", GDN_PREC_SOLVE)(t_inv, v * beta_col)
    w = _mm("nn", GDN_PREC_SOLVE)(t_inv, k_beta * e_dec)
    qk = _mm("nt", GDN_PREC)(q, k) * gamma
    v_new = u - _mm("nn", GDN_PREC)(w, state)
    o = _mm("nn", GDN_PREC)(q * e_dec, state) + _mm("nn", GDN_PREC)(qk, v_new)
    s_new = state * jnp.exp(decay_last) + _mm("tn", GDN_PREC)(k * jnp.exp(decay_last - decay_col), v_new)
    o_n = o * lax.rsqrt(jnp.mean(o * o, axis=-1, keepdims=True) + NORM_EPS) * norm_g
    return o_n * _silu(z), s_new


def _heads(ref, hb):
    return jnp.stack([ref[:, HEAD_DIM * i:HEAD_DIM * (i + 1)] for i in range(hb)], axis=0)


def _gdn_fwd(y, proj, a_rows, b_rows, a_log, dt_bias, norm_g, carried=None):
    T = y.shape[0]
    H, HB = GDN_HEADS, GDN_HB
    NC, HG, BW = T // CHUNK, GDN_HEADS // GDN_HB, GDN_HB * HEAD_DIM

    def body(q_ref, k_ref, v_ref, z_ref, a_ref, b_ref, alog_ref, dt_ref, ng_ref, o_ref, sst_ref, s_scr):
        @pl.when(pl.program_id(1) == 0)
        def _():
            s_scr[...] = jnp.zeros_like(s_scr)

        state = s_scr[...]
        sst_ref[...] = state
        o_g, s_new = _gdn_chunk(_heads(q_ref, HB), _heads(k_ref, HB), _heads(v_ref, HB), _heads(z_ref, HB),
                                a_ref[...], b_ref[...], alog_ref[...], dt_ref[...], ng_ref[...], state)
        s_scr[...] = s_new
        for i in range(HB):
            o_ref[:, HEAD_DIM * i:HEAD_DIM * (i + 1)] = o_g[i].astype(BF16)

    col = lambda off: pl.BlockSpec((CHUNK, BW), lambda hg, n, off=off: (n, off + hg))
    row = pl.BlockSpec((HB, None, 1, CHUNK), lambda hg, n: (hg, n, 0, 0))
    sc = pl.BlockSpec((HB, 1, 1), lambda hg, n: (hg, 0, 0))
    (o, states), moved = _call(
        body, (y, y, y, proj, a_rows, b_rows, a_log, dt_bias, norm_g), grid=(HG, NC),
        in_specs=[col(0), col(HG), col(2 * HG), col(Z0 // BW), row, row, sc, sc,
                  pl.BlockSpec((1, 1, HEAD_DIM), lambda hg, n: (0, 0, 0))],
        out_specs=[col(0), pl.BlockSpec((HB, None, HEAD_DIM, HEAD_DIM), lambda hg, n: (hg, n, 0, 0))],
        out_shape=[jax.ShapeDtypeStruct((T, GDN_W), BF16), jax.ShapeDtypeStruct((H, NC, HEAD_DIM, HEAD_DIM), F32)],
        scratch_shapes=[pltpu.VMEM((HB, HEAD_DIM, HEAD_DIM), F32)],
        sem=("parallel", "arbitrary"), name="gdn_fwd", carried=carried)
    return o, states, moved


def _gdn_bwd(y, proj, a_rows, b_rows, a_log, dt_bias, norm_g, states, do, carried=None):
    T = y.shape[0]
    H, HB = GDN_HEADS, GDN_HB
    NC, HG, BW = T // CHUNK, GDN_HEADS // GDN_HB, GDN_HB * HEAD_DIM

    def body(q_ref, k_ref, v_ref, z_ref, a_ref, b_ref, alog_ref, dt_ref, ng_ref, sst_ref, do_ref,
             dq_ref, dk_ref, dv_ref, dz_ref, da_ref, db_ref, dalog_ref, ddt_ref, dng_ref, ds_scr):
        hg, n = pl.program_id(0), pl.program_id(1)

        @pl.when(n == 0)
        def _():
            ds_scr[...] = jnp.zeros_like(ds_scr)
            dalog_ref[...] = jnp.zeros_like(dalog_ref)
            ddt_ref[...] = jnp.zeros_like(ddt_ref)

        @pl.when((n == 0) & (hg == 0))
        def _():
            dng_ref[...] = jnp.zeros_like(dng_ref)

        args = (_heads(q_ref, HB), _heads(k_ref, HB), _heads(v_ref, HB), _heads(z_ref, HB), a_ref[...], b_ref[...],
                alog_ref[...], dt_ref[...], ng_ref[...], sst_ref[...])
        _, vjp = jax.vjp(_gdn_chunk, *args)
        dq, dk, dv, dz, da, db, dalog, ddt, dng, d_state = vjp((_heads(do_ref, HB).astype(F32), ds_scr[...]))
        ds_scr[...] = d_state
        for i in range(HB):
            sl = slice(HEAD_DIM * i, HEAD_DIM * (i + 1))
            dq_ref[:, sl] = dq[i]
            dk_ref[:, sl] = dk[i]
            dv_ref[:, sl] = dv[i]
            dz_ref[:, sl] = dz[i].astype(BF16)
        da_ref[...] = da
        db_ref[...] = db
        dalog_ref[...] += dalog
        ddt_ref[...] += ddt
        dng_ref[...] += dng

    rev = lambda n: NC - 1 - n
    col = lambda off: pl.BlockSpec((CHUNK, BW), lambda hg, n, off=off: (rev(n), off + hg))
    row = pl.BlockSpec((HB, None, 1, CHUNK), lambda hg, n: (hg, rev(n), 0, 0))
    sc = pl.BlockSpec((HB, 1, 1), lambda hg, n: (hg, 0, 0))
    ng = pl.BlockSpec((1, 1, HEAD_DIM), lambda hg, n: (0, 0, 0))
    sst = pl.BlockSpec((HB, None, HEAD_DIM, HEAD_DIM), lambda hg, n: (hg, rev(n), 0, 0))
    rw = jax.ShapeDtypeStruct((H, NC, 1, CHUNK), F32)
    s1 = jax.ShapeDtypeStruct((H, 1, 1), F32)
    (dq, dk, dv, dz, da, db, dalog, ddt, dng), moved = _call(
        body, (y, y, y, proj, a_rows, b_rows, a_log, dt_bias, norm_g, states, do), grid=(HG, NC),
        in_specs=[col(0), col(HG), col(2 * HG), col(Z0 // BW), row, row, sc, sc, ng, sst, col(0)],
        out_specs=[col(0), col(0), col(0), col(0), row, row, sc, sc, ng],
        out_shape=[jax.ShapeDtypeStruct((T, GDN_W), F32)] * 3 + [jax.ShapeDtypeStruct((T, GDN_W), BF16), rw, rw, s1, s1,
                                                                jax.ShapeDtypeStruct((1, 1, HEAD_DIM), F32)],
        scratch_shapes=[pltpu.VMEM((HB, HEAD_DIM, HEAD_DIM), F32)],
        sem=("arbitrary", "arbitrary"), name="gdn_bwd", carried=carried)
    return jnp.concatenate([dq, dk, dv], axis=1), dz, da, db, dalog, ddt, dng, moved


def _swa_block(q, k_prev, k_cur, v_prev, v_cur, sink, slope, has_prev):
    kb = jnp.concatenate([k_prev, k_cur], axis=0)
    vb = jnp.concatenate([v_prev, v_cur], axis=0)
    q2 = q.reshape(SWA_GROUP * WINDOW, SWA_DIM)
    s = _mm("nt", SWA_PREC)(q2, kb)
    s = s.reshape(SWA_GROUP, WINDOW, 2 * WINDOW) * (SWA_DIM ** -0.5)
    qi = lax.broadcasted_iota(jnp.int32, (1, WINDOW, 2 * WINDOW), 1)
    sj = lax.broadcasted_iota(jnp.int32, (1, WINDOW, 2 * WINDOW), 2)
    dist = qi + WINDOW - sj
    valid = (dist >= 0) & (dist < WINDOW) & (has_prev | (sj >= WINDOW))
    s = jnp.where(valid, s - slope * dist.astype(F32), -jnp.inf)
    m = lax.stop_gradient(jnp.maximum(jnp.max(s, axis=-1, keepdims=True), sink))
    p = jnp.exp(s - m)
    probs = p / (jnp.sum(p, axis=-1, keepdims=True) + jnp.exp(sink - m))
    o = _mm("nn", SWA_PREC)(probs.reshape(SWA_GROUP * WINDOW, 2 * WINDOW), vb)
    return o.reshape(SWA_GROUP, WINDOW, SWA_DIM)


def _alibi_slopes():
    return (2.0 ** (-8.0 * jnp.arange(1, SWA_HEADS + 1, dtype=F32) / SWA_HEADS)).reshape(SWA_HEADS, 1, 1)


def _swa_fwd(q, k, v, sinks, carried=None):
    T = q.shape[1]
    NB = T // WINDOW

    def body(q_ref, kp_ref, kc_ref, vp_ref, vc_ref, sink_ref, slope_ref, o_ref):
        o = _swa_block(q_ref[...], kp_ref[...], kc_ref[...], vp_ref[...], vc_ref[...], sink_ref[...],
                       slope_ref[...], pl.program_id(1) > 0)
        o_ref[...] = o.astype(BF16)

    qs = pl.BlockSpec((SWA_GROUP, WINDOW, SWA_DIM), lambda h, n: (h, n, 0))
    cur = pl.BlockSpec((None, WINDOW, SWA_DIM), lambda h, n: (h, n, 0))
    prev = pl.BlockSpec((None, WINDOW, SWA_DIM), lambda h, n: (h, jnp.maximum(n - 1, 0), 0))
    hs = pl.BlockSpec((SWA_GROUP, 1, 1), lambda h, n: (h, 0, 0))
    (o,), moved = _call(
        body, (q, k, k, v, v, sinks, _alibi_slopes()), grid=(SWA_KV, NB),
        in_specs=[qs, prev, cur, prev, cur, hs, hs], out_specs=[qs],
        out_shape=[jax.ShapeDtypeStruct((SWA_HEADS, T, SWA_DIM), BF16)],
        sem=("parallel", "parallel"), name="swa_fwd", carried=carried)
    return o, moved


def _swa_bwd(q, k, v, sinks, do, carried=None):
    T = q.shape[1]
    NB = T // WINDOW

    def body(q_ref, kp_ref, kc_ref, vp_ref, vc_ref, sink_ref, slope_ref, do_ref,
             dq_ref, dk_ref, dv_ref, dsink_ref, dk_scr, dv_scr):
        n = pl.program_id(1)

        @pl.when(n == 0)
        def _():
            dk_scr[...] = jnp.zeros_like(dk_scr)
            dv_scr[...] = jnp.zeros_like(dv_scr)
            dsink_ref[...] = jnp.zeros_like(dsink_ref)

        has_prev = n < NB - 1
        fn = functools.partial(_swa_block, slope=slope_ref[...], has_prev=has_prev)
        _, vjp = jax.vjp(fn, q_ref[...], kp_ref[...], kc_ref[...], vp_ref[...], vc_ref[...], sink_ref[...])
        dq, dkp, dkc, dvp, dvc, dsink = vjp(do_ref[...].astype(F32))
        dq_ref[...] = dq.astype(BF16)
        dk_ref[...] = (dkc + dk_scr[...]).astype(BF16)
        dv_ref[...] = (dvc + dv_scr[...]).astype(BF16)
        dk_scr[...] = dkp
        dv_scr[...] = dvp
        dsink_ref[...] += dsink

    rev = lambda n: NB - 1 - n
    qs = pl.BlockSpec((SWA_GROUP, WINDOW, SWA_DIM), lambda h, n: (h, rev(n), 0))
    cur = pl.BlockSpec((None, WINDOW, SWA_DIM), lambda h, n: (h, rev(n), 0))
    prev = pl.BlockSpec((None, WINDOW, SWA_DIM), lambda h, n: (h, jnp.maximum(rev(n) - 1, 0), 0))
    hs = pl.BlockSpec((SWA_GROUP, 1, 1), lambda h, n: (h, 0, 0))
    kv = jax.ShapeDtypeStruct((SWA_KV, T, SWA_DIM), BF16)
    (dq, dk, dv, dsink), moved = _call(
        body, (q, k, k, v, v, sinks, _alibi_slopes(), do), grid=(SWA_KV, NB),
        in_specs=[qs, prev, cur, prev, cur, hs, hs, qs], out_specs=[qs, cur, cur, hs],
        out_shape=[jax.ShapeDtypeStruct((SWA_HEADS, T, SWA_DIM), BF16), kv, kv,
                   jax.ShapeDtypeStruct((SWA_HEADS, 1, 1), F32)],
        scratch_shapes=[pltpu.VMEM((WINDOW, SWA_DIM), F32), pltpu.VMEM((WINDOW, SWA_DIM), F32)],
        sem=("parallel", "arbitrary"), name="swa_bwd", carried=carried)
    return dq, dk, dv, dsink, moved


def _to_heads(t, n_heads):
    return t.reshape(t.shape[0], n_heads, SWA_DIM).transpose(1, 0, 2)


def _from_heads(t):
    return t.transpose(1, 0, 2).reshape(t.shape[1], -1)


def _rows(t):
    return t.T.reshape(t.shape[1], t.shape[0] // CHUNK, 1, CHUNK)


def _unrows(t):
    return t.reshape(t.shape[0], -1).T


def _branch_inputs(proj):
    b_rows = _rows(proj[:, BA0:BA0 + GDN_HEADS])
    a_rows = _rows(proj[:, BA0 + GDN_HEADS:BA0 + 2 * GDN_HEADS])
    q_s = _to_heads(proj[:, QS0:QS0 + D_MODEL], SWA_HEADS)
    k_s = _to_heads(proj[:, KS0:KS0 + KV_W], SWA_KV)
    v_s = _to_heads(proj[:, VS0:VS0 + KV_W], SWA_KV)
    return a_rows, b_rows, q_s, k_s, v_s


def _relu2_epilogue(acc):
    r = jnp.maximum(acc, 0.0)
    return acc, r * r


_GATHER_HOSTS = {"gdn_fwd": ("w_in", "conv_w"), "swa_fwd": ("w_branch_gdn", "w_branch_swa", "w_out"),
                 "mm_up": ("w_ff_up",), "mm_down": ("w_ff_down",)}
_SCATTER_HOSTS = {"swa_bwd": ("w_branch_gdn", "w_branch_swa", "w_out"), "gdn_bwd": ("w_ff_up", "w_ff_down")}
_SCATTER_LATE = ("w_in", "conv_w")


def _layer_fwd(x, w, next_shards=None):
    got = {}

    def carry(host):
        if next_shards is None:
            return None
        return _Exchange("gather", [next_shards[k] for k in _GATHER_HOSTS[host]])

    def keep(host, moved):
        if moved is not None:
            got.update(zip(_GATHER_HOSTS[host], moved))

    h1 = _rms_fwd(x, w["norm1_g"], "rms1_fwd")
    proj, = _matmul(h1, w["w_in"], "nn", [F32], name="mm_proj")
    a_rows, b_rows, q_s, k_s, v_s = _branch_inputs(proj)
    y = _conv_fwd(proj, w["conv_w"])
    o_gdn, states, moved = _gdn_fwd(y, proj, a_rows, b_rows, w["a_log"].reshape(-1, 1, 1),
                                    w["dt_bias"].reshape(-1, 1, 1), w["gdn_norm_g"].reshape(1, 1, -1),
                                    carried=carry("gdn_fwd"))
    keep("gdn_fwd", moved)
    o_swa, moved = _swa_fwd(q_s, k_s, v_s, w["attn_sinks"].reshape(-1, 1, 1), carried=carry("swa_fwd"))
    keep("swa_fwd", moved)
    o_swa = _from_heads(o_swa)
    y_gdn, = _matmul(o_gdn, w["w_branch_gdn"], "nn", [F32], name="mm_bgdn")
    y_swa, = _matmul(o_swa, w["w_branch_swa"], "nn", [F32], name="mm_bswa")
    mix = _gate_mix_fwd(proj, y_gdn, y_swa)
    x2, = _matmul(mix, w["w_out"], "nn", [F32], epilogue=lambda acc, r: (r + acc,), extras=(x,), name="mm_out")
    h2 = _rms_fwd(x2, w["norm2_g"], "rms2_fwd")
    u, act, *moved = _matmul(h2, w["w_ff_up"], "nn", [F32, BF16], epilogue=_relu2_epilogue, name="mm_up",
                             carried=carry("mm_up"))
    keep("mm_up", moved[0] if moved else None)
    x3, *moved = _matmul(act, w["w_ff_down"], "nn", [F32], epilogue=lambda acc, r: (r + acc,), extras=(x2,),
                         name="mm_down", carried=carry("mm_down"))
    keep("mm_down", moved[0] if moved else None)
    saved = dict(x=x, h1=h1, proj=proj, y=y, states=states, o_gdn=o_gdn, o_swa=o_swa, y_gdn=y_gdn, y_swa=y_swa,
                 mix=mix, x2=x2, h2=h2, u=u, act=act)
    return x3, saved, got


def _layer_bwd(dx3, w, s, late=None):
    T = dx3.shape[0]
    g, landed = {}, {}
    wdt = BF16

    def carry(host, extra=()):
        return _Exchange("scatter", [_block(g[k], _SHARDED[k]) for k in _SCATTER_HOSTS[host]] + list(extra))

    dx3b = dx3.astype(BF16)
    du, = _matmul(dx3b, w["w_ff_down"], "nt", [BF16], extras=(s["u"],),
                  epilogue=lambda acc, uu: (acc * 2.0 * jnp.maximum(uu, 0.0),), name="mm_down_dx")
    g["w_ff_down"], = _matmul(s["act"], dx3b, "tn", [wdt], name="mm_down_dw")
    dh2, = _matmul(du, w["w_ff_up"], "nt", [F32], name="mm_up_dx")
    g["w_ff_up"], = _matmul(s["h2"], du, "tn", [wdt], name="mm_up_dw")
    dx2, g["norm2_g"] = _rms_bwd(dh2, s["x2"], w["norm2_g"], dx3, "rms2_bwd")
    dx2b = dx2.astype(BF16)
    dmix, = _matmul(dx2b, w["w_out"], "nt", [F32], name="mm_out_dx")
    g["w_out"], = _matmul(s["mix"], dx2b, "tn", [wdt], name="mm_out_dw")
    dy_gdn, dy_swa, dgg, dgs = _gate_mix_bwd(dmix, s["proj"], s["y_gdn"], s["y_swa"])
    do_gdn, = _matmul(dy_gdn, w["w_branch_gdn"], "nt", [BF16], name="mm_bgdn_dx")
    g["w_branch_gdn"], = _matmul(s["o_gdn"], dy_gdn, "tn", [wdt], name="mm_bgdn_dw")
    do_swa, = _matmul(dy_swa, w["w_branch_swa"], "nt", [BF16], name="mm_bswa_dx")
    g["w_branch_swa"], = _matmul(s["o_swa"], dy_swa, "tn", [wdt], name="mm_bswa_dw")
    a_rows, b_rows, q_s, k_s, v_s = _branch_inputs(s["proj"])
    dq_s, dk_s, dv_s, dsink, moved = _swa_bwd(q_s, k_s, v_s, w["attn_sinks"].reshape(-1, 1, 1),
                                              _to_heads(do_swa, SWA_HEADS), carried=carry("swa_bwd"))
    landed.update(zip(_SCATTER_HOSTS["swa_bwd"], moved))
    g["attn_sinks"] = dsink.reshape(-1)
    dy, dz, da_rows, db_rows, dalog, ddt, dng, moved = _gdn_bwd(
        s["y"], s["proj"], a_rows, b_rows, w["a_log"].reshape(-1, 1, 1), w["dt_bias"].reshape(-1, 1, 1),
        w["gdn_norm_g"].reshape(1, 1, -1), s["states"], do_gdn, carried=carry("gdn_bwd", late or ()))
    n_own = len(_SCATTER_HOSTS["gdn_bwd"])
    landed.update(zip(_SCATTER_HOSTS["gdn_bwd"], moved[:n_own]))
    late_landed = moved[n_own:]
    g["a_log"], g["dt_bias"], g["gdn_norm_g"] = dalog.reshape(-1), ddt.reshape(-1), dng.reshape(-1)
    dqkv, g["conv_w"] = _conv_bwd(dy, s["proj"], w["conv_w"])
    dba = jnp.concatenate([_unrows(db_rows), _unrows(da_rows)], axis=1).astype(BF16)
    dproj = jnp.concatenate(
        [dqkv, dz, dgg, dgs, _from_heads(dq_s), _from_heads(dk_s), _from_heads(dv_s), dba,
         jnp.zeros((T, N_PROJ - BA0 - 2 * GDN_HEADS), BF16)], axis=1)
    dh1, = _matmul(dproj, w["w_in"], "nt", [F32], name="mm_proj_dx")
    g["w_in"], = _matmul(s["h1"], dproj, "tn", [wdt], name="mm_proj_dw")
    dx, g["norm1_g"] = _rms_bwd(dh1, s["x"], w["norm1_g"], dx2, "rms1_bwd")
    g["norm1_g"], g["norm2_g"] = g["norm1_g"].reshape(-1), g["norm2_g"].reshape(-1)
    own_late = [_block(_unalign_w_in(g["w_in"]), _SHARDED["w_in"]), _block(g["conv_w"], _SHARDED["conv_w"]).astype(BF16)]
    return dx, g, landed, late_landed, own_late


_LAYER_KEYS = ("norm1_g", "w_in", "conv_w", "a_log", "dt_bias", "gdn_norm_g", "attn_sinks", "w_branch_gdn",
               "w_branch_swa", "w_out", "norm2_g", "w_ff_up", "w_ff_down")


def _all_reduce_small(part):
    R = part.shape[0]

    def body(x_ref, sum_ref, gath, send_sems, recv_sems):
        me = _my_index()
        gath[me] = x_ref[...]
        sends = []
        for mask in range(1, N_DEV):
            dev, _ = _peer(mask)
            sends.append(pltpu.make_async_remote_copy(
                src_ref=x_ref, dst_ref=gath.at[me], send_sem=send_sems.at[mask - 1], recv_sem=recv_sems.at[mask - 1],
                device_id=dev, device_id_type=MESH))
        for cp in sends:
            cp.start()
        for mask in range(1, N_DEV):
            dev, idx = _peer(mask)
            pltpu.make_async_remote_copy(
                src_ref=x_ref, dst_ref=gath.at[idx], send_sem=send_sems.at[mask - 1], recv_sem=recv_sems.at[mask - 1],
                device_id=dev, device_id_type=MESH).wait_recv()
        for cp in sends:
            cp.wait_send()
        acc = gath[0]
        for i in range(1, N_DEV):
            acc = acc + gath[i]
        sum_ref[...] = acc

    vm = pl.BlockSpec(memory_space=pltpu.VMEM)
    return pl.pallas_call(
        body, in_specs=[vm], out_specs=vm, out_shape=jax.ShapeDtypeStruct((R, 128), F32),
        scratch_shapes=[pltpu.VMEM((N_DEV, R, 128), F32), pltpu.SemaphoreType.DMA((7,)), pltpu.SemaphoreType.DMA((7,))],
        name="small_all_reduce",
    )(part)


def _adam_math(w, g, m, v):
    m = ADAM_B1 * m + (1.0 - ADAM_B1) * g
    v = ADAM_B2 * v + (1.0 - ADAM_B2) * (g * g)
    m_hat = m / (1.0 - ADAM_B1 ** ADAM_STEP)
    v_hat = v / (1.0 - ADAM_B2 ** ADAM_STEP)
    delta = -ADAM_LR * (m_hat / (jnp.sqrt(v_hat) + ADAM_EPS) + ADAM_WD * w)
    return delta, m, v


def _adamw_sum(parts, w, m, v, name):
    n_layers = len(parts)
    R, Cc = parts[0].shape[1:]
    tr = R
    while tr * Cc * 4 > (1 << 20) and tr % 32 == 0:
        tr //= 2
    nblk = R // tr

    def body(*refs):
        p_refs = refs[:n_layers]
        w_ref, m_ref, v_ref, g_ref, d_ref, nm_ref, nv_ref = refs[n_layers:]
        for j in range(n_layers):
            @pl.when(pl.program_id(0) == j)
            def _(p_ref=p_refs[j]):
                g = p_ref[0].astype(F32)
                for i in range(1, N_DEV):
                    g = g + p_ref[i].astype(F32)
                d, nm, nv = _adam_math(w_ref[...], g, m_ref[...], v_ref[...])
                g_ref[...], d_ref[...], nm_ref[...], nv_ref[...] = g, d, nm, nv

    def part_spec(j):
        return pl.BlockSpec((N_DEV, tr, Cc), lambda l, i: (0, jnp.where(l == j, i, jnp.where(l < j, 0, nblk - 1)), 0))

    blk = pl.BlockSpec((tr, Cc), lambda l, i: (l * nblk + i, 0))
    out = jax.ShapeDtypeStruct((n_layers * R, Cc), F32)
    return pl.pallas_call(
        body, grid=(n_layers, nblk), in_specs=[part_spec(j) for j in range(n_layers)] + [blk, blk, blk],
        out_specs=[blk] * 4, out_shape=[out] * 4, compiler_params=_cp("arbitrary", "arbitrary"), name=name,
    )(*parts, w, m, v)


def _adamw_small(g, w, m, v):
    def body(g_ref, w_ref, m_ref, v_ref, d_ref, nm_ref, nv_ref):
        d_ref[...], nm_ref[...], nv_ref[...] = _adam_math(w_ref[...], g_ref[...], m_ref[...], v_ref[...])

    out = jax.ShapeDtypeStruct(g.shape, F32)
    return pl.pallas_call(body, out_shape=[out] * 3, name="adamw_small")(g, w, m, v)


def _align_w_in(w_full):
    by_dst = sorted(_SEGS, key=lambda s: s[2])
    parts = [w_full[:, src:src + width] for src, width, _ in by_dst]
    end = by_dst[-1][2] + by_dst[-1][1]
    return jnp.concatenate(parts + [jnp.zeros((w_full.shape[0], N_PROJ - end), w_full.dtype)], axis=1)


def _unalign_w_in(g):
    return jnp.concatenate([g[:, dst:dst + width] for _, width, dst in _SEGS], axis=1)


_SHARDED = {"w_in": 1, "conv_w": 1, "w_branch_gdn": 0, "w_branch_swa": 0, "w_out": 0, "w_ff_up": 1, "w_ff_down": 0}
_SMALL = ("norm1_g", "a_log", "dt_bias", "gdn_norm_g", "attn_sinks", "norm2_g", "final_norm_g")


def _unblock(gathered, axis):
    if axis == 0:
        return gathered.reshape(-1, gathered.shape[2])
    return gathered.transpose(1, 0, 2).reshape(gathered.shape[1], -1)


def _block(full, axis):
    A, B = full.shape
    if axis == 0:
        return full.reshape(N_DEV, A // N_DEV, B)
    return full.reshape(A, N_DEV, B // N_DEV).transpose(1, 0, 2)


def _pack_small(d, loss_row=None):
    rows = [d[k].astype(F32).reshape(-1, 128) if d[k].size % 128 == 0 else
            jnp.pad(d[k].astype(F32), ((0, 0), (0, 128 - d[k].shape[-1]))) for k in _SMALL]
    rows.append(jnp.zeros((1, 128), F32) if loss_row is None else loss_row)
    packed = jnp.concatenate(rows, axis=0)
    return jnp.pad(packed, ((0, -packed.shape[0] % 8), (0, 0)))


def _unpack_small(packed, like):
    out, r = {}, 0
    for k in _SMALL:
        shp = like[k].shape
        if like[k].size % 128 == 0:
            n = like[k].size // 128
            out[k] = packed[r:r + n].reshape(shp)
        else:
            n = shp[0]
            out[k] = packed[r:r + n, :shp[-1]]
        r += n
    return out, packed[r, 0]


def kernel(x, norm1_g, w_in, conv_w, a_log, dt_bias, gdn_norm_g, attn_sinks, w_branch_gdn, w_branch_swa, w_out, norm2_g, w_ff_up, w_ff_down, final_norm_g, loss_target, m_norm1_g, m_w_in, m_conv_w, m_a_log, m_dt_bias, m_gdn_norm_g, m_attn_sinks, m_w_branch_gdn, m_w_branch_swa, m_w_out, m_norm2_g, m_w_ff_up, m_w_ff_down, m_final_norm_g, v_norm1_g, v_w_in, v_conv_w, v_a_log, v_dt_bias, v_gdn_norm_g, v_attn_sinks, v_w_branch_gdn, v_w_branch_swa, v_w_out, v_norm2_g, v_w_ff_up, v_w_ff_down, v_final_norm_g):
    names = ("norm1_g", "w_in", "conv_w", "a_log", "dt_bias", "gdn_norm_g", "attn_sinks", "w_branch_gdn",
             "w_branch_swa", "w_out", "norm2_g", "w_ff_up", "w_ff_down", "final_norm_g")
    w = dict(zip(names, (norm1_g, w_in, conv_w, a_log, dt_bias, gdn_norm_g, attn_sinks, w_branch_gdn, w_branch_swa,
                         w_out, norm2_g, w_ff_up, w_ff_down, final_norm_g)))
    m = dict(zip(names, (m_norm1_g, m_w_in, m_conv_w, m_a_log, m_dt_bias, m_gdn_norm_g, m_attn_sinks, m_w_branch_gdn,
                         m_w_branch_swa, m_w_out, m_norm2_g, m_w_ff_up, m_w_ff_down, m_final_norm_g)))
    v = dict(zip(names, (v_norm1_g, v_w_in, v_conv_w, v_a_log, v_dt_bias, v_gdn_norm_g, v_attn_sinks, v_w_branch_gdn,
                         v_w_branch_swa, v_w_out, v_norm2_g, v_w_ff_up, v_w_ff_down, v_final_norm_g)))
    sharded = list(_SHARDED)

    shards = [{k: w[k][l] if k == "conv_w" else w[k][l].astype(BF16) for k in sharded} for l in range(DEPTH)]

    def layer_weights(l, gathered):
        wl = {k: _unblock(gathered[k], _SHARDED[k]) for k in sharded}
        wl["w_in"] = _align_w_in(wl["w_in"])
        wl.update({k: w[k][l] for k in _LAYER_KEYS if k not in wl})
        return wl

    gathered = dict(zip(sharded, _exchange("gather", [shards[0][k] for k in sharded], "weight_all_gather")))
    xc, layer_w, saved = x[0], [], []
    for l in range(DEPTH):
        layer_w.append(layer_weights(l, gathered))
        xc, s, gathered = _layer_fwd(xc, layer_w[l], shards[l + 1] if l + 1 < DEPTH else None)
        saved.append(s)
    loss_row, dx, dgf = _loss_head(xc, final_norm_g, loss_target[0])

    grads, landed, late = [None] * DEPTH, [None] * DEPTH, None
    for l in reversed(range(DEPTH)):
        dx, grads[l], landed[l], late_landed, late = _layer_bwd(dx, layer_w[l], saved[l], late)
        if l + 1 < DEPTH:
            landed[l + 1].update(zip(_SCATTER_LATE, late_landed))
    landed[0].update(zip(_SCATTER_LATE, _exchange("scatter", late, "grad_scatter")))
    grad_x = dx
    out_g, out_d, out_m, out_v = {}, {}, {}, {}
    for k in sharded:
        shp = w[k].shape
        flat = lambda t: t.reshape(-1, shp[-1])
        parts = [landed[l][k] for l in range(DEPTH)]
        if parts[0].shape[1] % 8:
            parts = [jnp.concatenate(parts, axis=1)]
        res = _adamw_sum(parts, flat(w[k]), flat(m[k]), flat(v[k]), "adamw_" + k)
        out_g[k], out_d[k], out_m[k], out_v[k] = [t.reshape(shp) for t in res]

    small_g = {k: jnp.stack([gl[k] for gl in grads]) for k in _SMALL if k != "final_norm_g"}
    small_g["final_norm_g"] = dgf.reshape(-1)
    total = _all_reduce_small(_pack_small(small_g, loss_row))
    sd, sm, sv = _adamw_small(total, _pack_small(w), _pack_small(m), _pack_small(v))
    g_small, loss = _unpack_small(total, w)
    d_small, _ = _unpack_small(sd, w)
    m_small, _ = _unpack_small(sm, w)
    v_small, _ = _unpack_small(sv, w)
    out_g.update(g_small), out_d.update(d_small), out_m.update(m_small), out_v.update(v_small)

    return (loss, grad_x[None], *[out_g[k] for k in names], *[out_d[k] for k in names],
            *[out_m[k] for k in names], *[out_v[k] for k in names])
```

```python
import functools

import jax
import jax.numpy as jnp
from jax import lax
from jax.experimental import pallas as pl
from jax.experimental.pallas import tpu as pltpu

F32 = jnp.float32
BF16 = jnp.bfloat16
HI = lax.Precision.HIGHEST
MESH = pl.DeviceIdType.MESH

N_DEV = 8
D_MODEL = 2048
DEPTH = 4
GDN_HEADS = 16
HEAD_DIM = 128
CHUNK = 64
CONV_K = 4
SWA_HEADS = 32
SWA_KV = 4
SWA_GROUP = SWA_HEADS // SWA_KV
SWA_DIM = 64
WINDOW = 128
NORM_EPS = 1e-6
GDN_W = GDN_HEADS * HEAD_DIM
KV_W = SWA_KV * SWA_DIM

QKV0, Z0, GG0, GS0, QS0, KS0, VS0, BA0 = 0, 6144, 8192, 10240, 12288, 14336, 14592, 14848
N_PROJ = 15360
_SEGS = ((0, 6144, QKV0), (6144, 2048, Z0), (8192, 32, BA0), (8224, 2048, QS0), (10272, 256, KS0),
         (10528, 256, VS0), (10784, 4096, GG0))

ADAM_LR, ADAM_B1, ADAM_B2, ADAM_EPS, ADAM_WD, ADAM_STEP = 0.001, 0.9, 0.999, 1e-08, 0.01, 10

VMEM_LIMIT = 56 * 1024 * 1024
GDN_HB = 16
GDN_PREC_SOLVE = "x3"
GDN_PREC = "bf16"
SWA_PREC = "bf16"


def _cp(*sem):
    return pltpu.CompilerParams(dimension_semantics=sem, vmem_limit_bytes=VMEM_LIMIT)


def _tile(n, want):
    t = min(n, want)
    while n % t:
        t -= 128
    return t


def _my_index():
    return 4 * lax.axis_index("x") + 2 * lax.axis_index("y") + lax.axis_index("c")


def _peer(mask):
    x, y, c = lax.axis_index("x"), lax.axis_index("y"), lax.axis_index("c")
    px = 1 - x if mask & 4 else x
    py = 1 - y if mask & 2 else y
    pc = 1 - c if mask & 1 else c
    return (px, py, pc), 4 * px + 2 * py + pc


class _Exchange:
    def __init__(self, kind, arrays):
        self.kind, self.arrays, self.n = kind, list(arrays), len(arrays)
        self.out_shape = [jax.ShapeDtypeStruct(((N_DEV,) + a.shape) if kind == "gather" else a.shape, a.dtype)
                          for a in self.arrays]
        self.scratch = [pltpu.SemaphoreType.DMA((self.n, 7)), pltpu.SemaphoreType.DMA((self.n, 7)),
                        pltpu.SemaphoreType.DMA((self.n,))]

    def _local(self, ins, outs, sems):
        me = _my_index()
        src = (lambda a: ins[a]) if self.kind == "gather" else (lambda a: ins[a].at[me])
        return [pltpu.make_async_copy(src(a), outs[a].at[me], sems[2].at[a]) for a in range(self.n)]

    def _copy(self, outs, sems, a, k, src, block, to):
        return pltpu.make_async_remote_copy(src_ref=src, dst_ref=outs[a].at[block], send_sem=sems[0].at[a, k],
                                            recv_sem=sems[1].at[a, k], device_id=to, device_id_type=MESH)

    def _sends(self, ins, outs, sems):
        me = _my_index()
        cps = []
        for a in range(self.n):
            if self.kind == "gather":
                for k, mask in enumerate((1, 4, 2, 6)):
                    cps.append(self._copy(outs, sems, a, k, ins[a], me, _peer(mask)[0]))
            else:
                for mask in range(1, N_DEV):
                    dev, idx = _peer(mask)
                    cps.append(self._copy(outs, sems, a, mask - 1, ins[a].at[idx], me, dev))
        return cps

    def _relays(self, outs, sems):
        sibling = _peer(1)[0]
        return [self._copy(outs, sems, a, 4 + j, outs[a].at[_peer(mask)[1]], _peer(mask)[1], sibling)
                for a in range(self.n) for j, mask in enumerate((4, 2, 6))]

    def start(self, ins, outs, sems):
        for cp in self._local(ins, outs, sems) + self._sends(ins, outs, sems):
            cp.start()

    def finish(self, ins, outs, sems):
        me = _my_index()
        if self.kind == "gather":
            relays = self._relays(outs, sems)
            for a in range(self.n):
                for j, mask in enumerate((4, 2, 6)):
                    self._copy(outs, sems, a, 1 + j, ins[a], _peer(mask)[1], _peer(mask)[0]).wait_recv()
                    relays[3 * a + j].start()
            for a in range(self.n):
                self._copy(outs, sems, a, 0, ins[a], _peer(1)[1], _peer(1)[0]).wait_recv()
                for j, mask in enumerate((4, 2, 6)):
                    self._copy(outs, sems, a, 4 + j, ins[a], _peer(mask)[1] ^ 1, _peer(1)[0]).wait_recv()
            for cp in relays:
                cp.wait_send()
        else:
            for a in range(self.n):
                for mask in range(1, N_DEV):
                    dev, idx = _peer(mask)
                    self._copy(outs, sems, a, mask - 1, ins[a].at[idx], idx, dev).wait_recv()
        for cp in self._sends(ins, outs, sems):
            cp.wait_send()
        for cp in self._local(ins, outs, sems):
            cp.wait()


def _exchange(kind, arrays, name):
    ex = _Exchange(kind, arrays)

    def body(*refs):
        ins, outs, sems = refs[:ex.n], refs[ex.n:2 * ex.n], refs[2 * ex.n:]
        ex.start(ins, outs, sems)
        ex.finish(ins, outs, sems)

    any_spec = pl.BlockSpec(memory_space=pl.ANY)
    return pl.pallas_call(body, in_specs=[any_spec] * ex.n, out_specs=[any_spec] * ex.n, out_shape=ex.out_shape,
                          scratch_shapes=ex.scratch, name=name)(*ex.arrays)


def _call(body, args, *, grid, in_specs, out_specs, out_shape, scratch_shapes=(), sem, name, carried=None):
    if carried is None:
        outs = pl.pallas_call(body, grid=grid, in_specs=list(in_specs), out_specs=list(out_specs),
                              out_shape=list(out_shape), scratch_shapes=list(scratch_shapes),
                              compiler_params=_cp(*sem), name=name)(*args)
        return outs, None
    n_in, n_out, n_scr, n_c = len(args), len(out_shape), len(scratch_shapes), carried.n

    def wrapped(*refs):
        ins, c_in = refs[:n_in], refs[n_in:n_in + n_c]
        o0 = n_in + n_c
        outs, c_out = refs[o0:o0 + n_out], refs[o0 + n_out:o0 + n_out + n_c]
        s0 = o0 + n_out + n_c
        scr, c_sems = refs[s0:s0 + n_scr], refs[s0 + n_scr:]
        ids = [pl.program_id(i) for i in range(len(grid))]
        first = functools.reduce(jnp.logical_and, [i == 0 for i in ids])
        last = functools.reduce(jnp.logical_and, [i == g - 1 for i, g in zip(ids, grid)])

        @pl.when(first)
        def _():
            carried.start(c_in, c_out, c_sems)

        body(*ins, *outs, *scr)

        @pl.when(last)
        def _():
            carried.finish(c_in, c_out, c_sems)

    any_spec = pl.BlockSpec(memory_space=pl.ANY)
    res = pl.pallas_call(
        wrapped, grid=grid, in_specs=list(in_specs) + [any_spec] * n_c, out_specs=list(out_specs) + [any_spec] * n_c,
        out_shape=list(out_shape) + carried.out_shape, scratch_shapes=list(scratch_shapes) + carried.scratch,
        compiler_params=_cp(*["arbitrary"] * len(grid)), name=name)(*args, *carried.arrays)
    return res[:n_out], res[n_out:]


def _matmul(a, b, mode, out_shapes, epilogue=None, extras=(), name="mm", carried=None):
    if mode == "tn":
        K, M = a.shape
    else:
        M, K = a.shape
    N = b.shape[0] if mode == "nt" else b.shape[1]
    tm, tn, tk = _tile(M, 1024), _tile(N, 1024), _tile(K, 2048)
    nk = K // tk
    dims = {"nn": (((1,), (0,)), ((), ())), "nt": (((1,), (1,)), ((), ())), "tn": (((0,), (0,)), ((), ()))}[mode]
    n_ex, n_out = len(extras), len(out_shapes)

    def body(*refs):
        a_ref, b_ref = refs[:2]
        ex = refs[2:2 + n_ex]
        outs = refs[2 + n_ex:2 + n_ex + n_out]
        k = pl.program_id(2)

        def product():
            return lax.dot_general(a_ref[...].astype(BF16), b_ref[...].astype(BF16), dims,
                                   preferred_element_type=F32)

        def finish(total):
            res = (total,) if epilogue is None else epilogue(total, *[e[...] for e in ex])
            for o, r in zip(outs, res):
                o[...] = r.astype(o.dtype)

        if nk == 1:
            finish(product())
            return
        acc = refs[-1]

        @pl.when(k == 0)
        def _():
            acc[...] = product()

        @pl.when((k > 0) & (k < nk - 1))
        def _():
            acc[...] += product()

        @pl.when(k == nk - 1)
        def _():
            finish(acc[...] + product())

    a_spec = pl.BlockSpec((tk, tm), lambda i, j, k: (k, i)) if mode == "tn" else pl.BlockSpec((tm, tk), lambda i, j, k: (i, k))
    b_spec = pl.BlockSpec((tn, tk), lambda i, j, k: (j, k)) if mode == "nt" else pl.BlockSpec((tk, tn), lambda i, j, k: (k, j))
    mn = pl.BlockSpec((tm, tn), lambda i, j, k: (i, j))
    outs, moved = _call(
        body, (a, b, *extras), grid=(M // tm, N // tn, nk),
        in_specs=[a_spec, b_spec] + [mn] * n_ex, out_specs=[mn] * n_out,
        out_shape=[jax.ShapeDtypeStruct((M, N), dt) for dt in out_shapes],
        scratch_shapes=[pltpu.VMEM((tm, tn), F32)] if nk > 1 else [],
        sem=("parallel", "parallel", "arbitrary"), name=name, carried=carried)
    return list(outs) if carried is None else list(outs) + [moved]


def _rms_fwd(x, g, name):
    T, D = x.shape
    tr = _tile(T, 256)

    def body(x_ref, g_ref, h_ref):
        xv = x_ref[...]
        r = lax.rsqrt(jnp.mean(xv * xv, axis=-1, keepdims=True) + NORM_EPS)
        h_ref[...] = (xv * r * g_ref[...]).astype(BF16)

    return pl.pallas_call(
        body, grid=(T // tr,),
        in_specs=[pl.BlockSpec((tr, D), lambda i: (i, 0)), pl.BlockSpec((1, D), lambda i: (0, 0))],
        out_specs=pl.BlockSpec((tr, D), lambda i: (i, 0)),
        out_shape=jax.ShapeDtypeStruct((T, D), BF16), compiler_params=_cp("parallel"), name=name,
    )(x, g.reshape(1, D))


def _rms_bwd(dh, x, g, dres, name):
    T, D = x.shape
    tr = _tile(T, 256)

    def body(dh_ref, x_ref, g_ref, dres_ref, dx_ref, dg_ref):
        @pl.when(pl.program_id(0) == 0)
        def _():
            dg_ref[...] = jnp.zeros_like(dg_ref)

        xv = x_ref[...]
        r = lax.rsqrt(jnp.mean(xv * xv, axis=-1, keepdims=True) + NORM_EPS)
        xhat = xv * r
        dhv = dh_ref[...].astype(F32)
        gd = dhv * g_ref[...]
        dx_ref[...] = dres_ref[...] + r * (gd - xhat * jnp.mean(gd * xhat, axis=-1, keepdims=True))
        dg_ref[...] += jnp.sum(dhv * xhat, axis=0, keepdims=True)

    row = pl.BlockSpec((tr, D), lambda i: (i, 0))
    vec = pl.BlockSpec((1, D), lambda i: (0, 0))
    return pl.pallas_call(
        body, grid=(T // tr,), in_specs=[row, row, vec, row], out_specs=[row, vec],
        out_shape=[jax.ShapeDtypeStruct((T, D), F32), jax.ShapeDtypeStruct((1, D), F32)],
        compiler_params=_cp("arbitrary"), name=name,
    )(dh, x, g.reshape(1, D), dres)


def _loss_head(x, g, tgt):
    T, D = x.shape
    tr = _tile(T, 256)

    def body(x_ref, g_ref, t_ref, loss_ref, dx_ref, dg_ref):
        @pl.when(pl.program_id(0) == 0)
        def _():
            dg_ref[...] = jnp.zeros_like(dg_ref)
            loss_ref[...] = jnp.zeros_like(loss_ref)

        xv = x_ref[...]
        r = lax.rsqrt(jnp.mean(xv * xv, axis=-1, keepdims=True) + NORM_EPS)
        xhat = xv * r
        err = xhat * g_ref[...] - t_ref[...]
        loss_ref[...] += (0.5 / D) * jnp.sum(jnp.sum(err * err, axis=-1, keepdims=True), axis=0, keepdims=True)
        dy = err * (1.0 / D)
        gd = dy * g_ref[...]
        dx_ref[...] = r * (gd - xhat * jnp.mean(gd * xhat, axis=-1, keepdims=True))
        dg_ref[...] += jnp.sum(dy * xhat, axis=0, keepdims=True)

    row = pl.BlockSpec((tr, D), lambda i: (i, 0))
    vec = pl.BlockSpec((1, D), lambda i: (0, 0))
    return pl.pallas_call(
        body, grid=(T // tr,), in_specs=[row, vec, row],
        out_specs=[pl.BlockSpec((1, 128), lambda i: (0, 0)), row, vec],
        out_shape=[jax.ShapeDtypeStruct((1, 128), F32), jax.ShapeDtypeStruct((T, D), F32),
                   jax.ShapeDtypeStruct((1, D), F32)],
        compiler_params=_cp("arbitrary"), name="loss_head",
    )(x, g.reshape(1, D), tgt)


def _sigmoid(x):
    return 1.0 / (1.0 + jnp.exp(-x))


def _gate_mix_fwd(proj, y_gdn, y_swa):
    T = proj.shape[0]
    tr, tc = _tile(T, 512), 512
    nc = D_MODEL // tc

    def body(gg_ref, gs_ref, yg_ref, ys_ref, mix_ref):
        mix_ref[...] = (_sigmoid(gg_ref[...]) * yg_ref[...] + _sigmoid(gs_ref[...]) * ys_ref[...]).astype(BF16)

    blk = pl.BlockSpec((tr, tc), lambda i, j: (i, j))
    return pl.pallas_call(
        body, grid=(T // tr, nc),
        in_specs=[pl.BlockSpec((tr, tc), lambda i, j: (i, GG0 // tc + j)),
                  pl.BlockSpec((tr, tc), lambda i, j: (i, GS0 // tc + j)), blk, blk],
        out_specs=blk, out_shape=jax.ShapeDtypeStruct((T, D_MODEL), BF16),
        compiler_params=_cp("parallel", "parallel"), name="gate_mix_fwd",
    )(proj, proj, y_gdn, y_swa)


def _gate_mix_bwd(dmix, proj, y_gdn, y_swa):
    T = proj.shape[0]
    tr, tc = _tile(T, 512), 512
    nc = D_MODEL // tc

    def body(dm_ref, gg_ref, gs_ref, yg_ref, ys_ref, dyg_ref, dys_ref, dgg_ref, dgs_ref):
        dm = dm_ref[...]
        sg, ss = _sigmoid(gg_ref[...]), _sigmoid(gs_ref[...])
        dyg_ref[...] = (dm * sg).astype(BF16)
        dys_ref[...] = (dm * ss).astype(BF16)
        dgg_ref[...] = (dm * yg_ref[...] * sg * (1.0 - sg)).astype(BF16)
        dgs_ref[...] = (dm * ys_ref[...] * ss * (1.0 - ss)).astype(BF16)

    blk = pl.BlockSpec((tr, tc), lambda i, j: (i, j))
    out = jax.ShapeDtypeStruct((T, D_MODEL), BF16)
    return pl.pallas_call(
        body, grid=(T // tr, nc),
        in_specs=[blk, pl.BlockSpec((tr, tc), lambda i, j: (i, GG0 // tc + j)),
                  pl.BlockSpec((tr, tc), lambda i, j: (i, GS0 // tc + j)), blk, blk],
        out_specs=[blk] * 4, out_shape=[out] * 4,
        compiler_params=_cp("parallel", "parallel"), name="gate_mix_bwd",
    )(dmix, proj, proj, y_gdn, y_swa)


CONV_TT, CONV_TW, HALO = 512, 256, 8


def _conv_fwd(proj, conv_w, carried=None):
    T = proj.shape[0]
    tt = _tile(T, CONV_TT)
    W = 3 * GDN_W

    def body(x_ref, halo_ref, w_ref, y_ref):
        first = pl.program_id(1) == 0
        halo = jnp.where(first, 0.0, halo_ref[...])
        xe = jnp.concatenate([halo, x_ref[...]], axis=0)
        acc = xe[HALO:] * w_ref[pl.ds(CONV_K - 1, 1), :]
        for j in range(CONV_K - 1):
            acc = acc + pltpu.roll(xe, CONV_K - 1 - j, 0)[HALO:] * w_ref[pl.ds(j, 1), :]
        y_ref[...] = acc

    (y,), moved = _call(
        body, (proj, proj, conv_w), grid=(W // CONV_TW, T // tt),
        in_specs=[pl.BlockSpec((tt, CONV_TW), lambda c, t: (t, c)),
                  pl.BlockSpec((HALO, CONV_TW), lambda c, t: (jnp.maximum(t * (tt // HALO) - 1, 0), c)),
                  pl.BlockSpec((CONV_K, CONV_TW), lambda c, t: (0, c))],
        out_specs=[pl.BlockSpec((tt, CONV_TW), lambda c, t: (t, c))],
        out_shape=[jax.ShapeDtypeStruct((T, W), F32)],
        sem=("parallel", "parallel"), name="conv_fwd", carried=carried)
    return y, moved


def _conv_bwd(dy, proj, conv_w, carried=None):
    T = proj.shape[0]
    tt = _tile(T, CONV_TT)
    nt = T // tt
    W = 3 * GDN_W

    def body(dy_ref, dnext_ref, x_ref, halo_ref, w_ref, dx_ref, dw_ref):
        t = pl.program_id(1)

        @pl.when(t == 0)
        def _():
            dw_ref[...] = jnp.zeros_like(dw_ref)

        dyv = dy_ref[...]
        dye = jnp.concatenate([dyv, jnp.where(t == nt - 1, 0.0, dnext_ref[...])], axis=0)
        xe = jnp.concatenate([jnp.where(t == 0, 0.0, halo_ref[...]), x_ref[...]], axis=0)
        acc = dyv * w_ref[pl.ds(CONV_K - 1, 1), :]
        dw_ref[pl.ds(CONV_K - 1, 1), :] += jnp.sum(dyv * xe[HALO:], axis=0, keepdims=True)
        for s in range(1, CONV_K):
            j = CONV_K - 1 - s
            acc = acc + pltpu.roll(dye, tt + HALO - s, 0)[:tt] * w_ref[pl.ds(j, 1), :]
            dw_ref[pl.ds(j, 1), :] += jnp.sum(dyv * pltpu.roll(xe, s, 0)[HALO:], axis=0, keepdims=True)
        dx_ref[...] = acc.astype(BF16)

    cur = pl.BlockSpec((tt, CONV_TW), lambda c, t: (t, c))
    per_plane = GDN_W // CONV_TW
    (dx, dw), moved = _call(
        body, (dy, dy, proj, proj, conv_w), grid=(W // CONV_TW, nt),
        in_specs=[pl.BlockSpec((None, tt, CONV_TW), lambda c, t: (c // per_plane, t, c % per_plane)),
                  pl.BlockSpec((None, HALO, CONV_TW),
                               lambda c, t: (c // per_plane, jnp.minimum((t + 1) * (tt // HALO), T // HALO - 1),
                                             c % per_plane)),
                  cur,
                  pl.BlockSpec((HALO, CONV_TW), lambda c, t: (jnp.maximum(t * (tt // HALO) - 1, 0), c)),
                  pl.BlockSpec((CONV_K, CONV_TW), lambda c, t: (0, c))],
        out_specs=[cur, pl.BlockSpec((CONV_K, CONV_TW), lambda c, t: (0, c))],
        out_shape=[jax.ShapeDtypeStruct((T, W), BF16), jax.ShapeDtypeStruct((CONV_K, W), F32)],
        sem=("parallel", "arbitrary"), name="conv_bwd", carried=carried)
    return dx, dw, moved


def _dot(a, b, kind, prec):
    nb = a.ndim - 2
    batch = tuple(range(nb))
    ca = nb if kind == "tn" else nb + 1
    cb = nb + 1 if kind == "nt" else nb
    dims = (((ca,), (cb,)), (batch, batch))
    if prec == "bf16":
        return lax.dot_general(a.astype(BF16), b.astype(BF16), dims, preferred_element_type=F32)
    return lax.dot_general(a, b, dims, precision=HI if prec == "f32" else lax.Precision.HIGH,
                           preferred_element_type=F32)


@functools.lru_cache(maxsize=None)
def _mm(kind, prec):
    @jax.custom_vjp
    def f(a, b):
        return _dot(a, b, kind, prec)

    def fwd(a, b):
        return f(a, b), (a, b)

    def bwd(res, ct):
        a, b = res
        if kind == "nn":
            return _dot(ct, b, "nt", prec), _dot(a, ct, "tn", prec)
        if kind == "nt":
            return _dot(ct, b, "nn", prec), _dot(ct, a, "tn", prec)
        return _dot(b, ct, "nt", prec), _dot(a, ct, "nn", prec)

    f.defvjp(fwd, bwd)
    return f


def _col_from_row(row, eye):
    return jnp.sum(jnp.where(eye, row, 0.0), axis=2, keepdims=True)


def _silu(x):
    return x / (1.0 + jnp.exp(-x))


def _gdn_chunk(yq, yk, yv, z, a_row, b_row, a_log, dt_bias, norm_g, state):
    h = yq.shape[0]
    ii = lax.broadcasted_iota(jnp.int32, (1, CHUNK, CHUNK), 1)
    jj = lax.broadcasted_iota(jnp.int32, (1, CHUNK, CHUNK), 2)
    eye = ii == jj
    qr, kr, v = _silu(yq), _silu(yk), _silu(yv)
    q = qr * lax.rsqrt(jnp.sum(qr * qr, axis=-1, keepdims=True) + NORM_EPS) * (HEAD_DIM ** -0.5)
    k = kr * lax.rsqrt(jnp.sum(kr * kr, axis=-1, keepdims=True) + NORM_EPS)
    beta_row = _sigmoid(b_row)
    xa = a_row + dt_bias
    g_row = -jnp.exp(a_log) * (jnp.maximum(xa, 0.0) + jnp.log(1.0 + jnp.exp(-jnp.abs(xa))))
    upper = jnp.broadcast_to(jnp.where(ii <= jj, 1.0, 0.0).astype(F32), (h, CHUNK, CHUNK))
    decay_row = _mm("nn", "f32")(g_row, upper)
    decay_col = _col_from_row(decay_row, eye)
    beta_col = _col_from_row(beta_row, eye)
    decay_last = jnp.sum(g_row, axis=2, keepdims=True)
    gamma = jnp.exp(jnp.where(ii >= jj, decay_col - decay_row, -jnp.inf))
    k_beta = k * beta_col
    a_low = jnp.where(ii > jj, _mm("nt", GDN_PREC_SOLVE)(k_beta, k) * gamma, 0.0)
    t_inv = jnp.where(eye, 1.0, 0.0).astype(F32) - a_low
    p = a_low
    for _ in range(5):
        p = _mm("nn", GDN_PREC_SOLVE)(p, p)
        t_inv = t_inv + _mm("nn", GDN_PREC_SOLVE)(t_inv, p)
    e_dec = jnp.exp(decay_col)
    u = _mm("nn", GDN_PREC_SOLVE)(t_inv, v * beta_col)
    w = _mm("nn", GDN_PREC_SOLVE)(t_inv, k_beta * e_dec)
    qk = _mm("nt", GDN_PREC)(q, k) * gamma
    v_new = u - _mm("nn", GDN_PREC)(w, state)
    o = _mm("nn", GDN_PREC)(q * e_dec, state) + _mm("nn", GDN_PREC)(qk, v_new)
    s_new = state * jnp.exp(decay_last) + _mm("tn", GDN_PREC)(k * jnp.exp(decay_last - decay_col), v_new)
    o_n = o * lax.rsqrt(jnp.mean(o * o, axis=-1, keepdims=True) + NORM_EPS) * norm_g
    return o_n * _silu(z), s_new


def _heads(ref, hb):
    return jnp.stack([ref[:, HEAD_DIM * i:HEAD_DIM * (i + 1)] for i in range(hb)], axis=0)


def _gdn_fwd(y, proj, a_rows, b_rows, a_log, dt_bias, norm_g, carried=None):
    T = y.shape[0]
    H, HB = GDN_HEADS, GDN_HB
    NC, HG, BW = T // CHUNK, GDN_HEADS // GDN_HB, GDN_HB * HEAD_DIM

    def body(q_ref, k_ref, v_ref, z_ref, a_ref, b_ref, alog_ref, dt_ref, ng_ref, o_ref, sst_ref, s_scr):
        @pl.when(pl.program_id(1) == 0)
        def _():
            s_scr[...] = jnp.zeros_like(s_scr)

        state = s_scr[...]
        sst_ref[...] = state
        o_g, s_new = _gdn_chunk(_heads(q_ref, HB), _heads(k_ref, HB), _heads(v_ref, HB), _heads(z_ref, HB),
                                a_ref[...], b_ref[...], alog_ref[...], dt_ref[...], ng_ref[...], state)
        s_scr[...] = s_new
        for i in range(HB):
            o_ref[:, HEAD_DIM * i:HEAD_DIM * (i + 1)] = o_g[i].astype(BF16)

    col = lambda off: pl.BlockSpec((CHUNK, BW), lambda hg, n, off=off: (n, off + hg))
    row = pl.BlockSpec((HB, None, 1, CHUNK), lambda hg, n: (hg, n, 0, 0))
    sc = pl.BlockSpec((HB, 1, 1), lambda hg, n: (hg, 0, 0))
    (o, states), moved = _call(
        body, (y, y, y, proj, a_rows, b_rows, a_log, dt_bias, norm_g), grid=(HG, NC),
        in_specs=[col(0), col(HG), col(2 * HG), col(Z0 // BW), row, row, sc, sc,
                  pl.BlockSpec((1, 1, HEAD_DIM), lambda hg, n: (0, 0, 0))],
        out_specs=[col(0), pl.BlockSpec((HB, None, HEAD_DIM, HEAD_DIM), lambda hg, n: (hg, n, 0, 0))],
        out_shape=[jax.ShapeDtypeStruct((T, GDN_W), BF16), jax.ShapeDtypeStruct((H, NC, HEAD_DIM, HEAD_DIM), F32)],
        scratch_shapes=[pltpu.VMEM((HB, HEAD_DIM, HEAD_DIM), F32)],
        sem=("parallel", "arbitrary"), name="gdn_fwd", carried=carried)
    return o, states, moved


def _gdn_bwd(y, proj, a_rows, b_rows, a_log, dt_bias, norm_g, states, do, carried=None):
    T = y.shape[0]
    H, HB = GDN_HEADS, GDN_HB
    NC, HG, BW = T // CHUNK, GDN_HEADS // GDN_HB, GDN_HB * HEAD_DIM

    def body(q_ref, k_ref, v_ref, z_ref, a_ref, b_ref, alog_ref, dt_ref, ng_ref, sst_ref, do_ref,
             dy_ref, dz_ref, da_ref, db_ref, dalog_ref, ddt_ref, dng_ref, ds_scr):
        hg, n = pl.program_id(0), pl.program_id(1)

        @pl.when(n == 0)
        def _():
            ds_scr[...] = jnp.zeros_like(ds_scr)
            dalog_ref[...] = jnp.zeros_like(dalog_ref)
            ddt_ref[...] = jnp.zeros_like(ddt_ref)

        @pl.when((n == 0) & (hg == 0))
        def _():
            dng_ref[...] = jnp.zeros_like(dng_ref)

        args = (_heads(q_ref, HB), _heads(k_ref, HB), _heads(v_ref, HB), _heads(z_ref, HB), a_ref[...], b_ref[...],
                alog_ref[...], dt_ref[...], ng_ref[...], sst_ref[...])
        _, vjp = jax.vjp(_gdn_chunk, *args)
        dq, dk, dv, dz, da, db, dalog, ddt, dng, d_state = vjp((_heads(do_ref, HB).astype(F32), ds_scr[...]))
        ds_scr[...] = d_state
        for i in range(HB):
            sl = slice(HEAD_DIM * i, HEAD_DIM * (i + 1))
            dy_ref[0, :, sl] = dq[i]
            dy_ref[1, :, sl] = dk[i]
            dy_ref[2, :, sl] = dv[i]
            dz_ref[:, sl] = dz[i].astype(BF16)
        da_ref[...] = da
        db_ref[...] = db
        dalog_ref[...] += dalog
        ddt_ref[...] += ddt
        dng_ref[...] += dng

    rev = lambda n: NC - 1 - n
    col = lambda off: pl.BlockSpec((CHUNK, BW), lambda hg, n, off=off: (rev(n), off + hg))
    row = pl.BlockSpec((HB, None, 1, CHUNK), lambda hg, n: (hg, rev(n), 0, 0))
    sc = pl.BlockSpec((HB, 1, 1), lambda hg, n: (hg, 0, 0))
    ng = pl.BlockSpec((1, 1, HEAD_DIM), lambda hg, n: (0, 0, 0))
    sst = pl.BlockSpec((HB, None, HEAD_DIM, HEAD_DIM), lambda hg, n: (hg, rev(n), 0, 0))
    rw = jax.ShapeDtypeStruct((H, NC, 1, CHUNK), F32)
    s1 = jax.ShapeDtypeStruct((H, 1, 1), F32)
    (dy, dz, da, db, dalog, ddt, dng), moved = _call(
        body, (y, y, y, proj, a_rows, b_rows, a_log, dt_bias, norm_g, states, do), grid=(HG, NC),
        in_specs=[col(0), col(HG), col(2 * HG), col(Z0 // BW), row, row, sc, sc, ng, sst, col(0)],
        out_specs=[pl.BlockSpec((3, CHUNK, BW), lambda hg, n: (0, rev(n), hg)), col(0), row, row, sc, sc, ng],
        out_shape=[jax.ShapeDtypeStruct((3, T, GDN_W), F32), jax.ShapeDtypeStruct((T, GDN_W), BF16), rw, rw, s1, s1,
                   jax.ShapeDtypeStruct((1, 1, HEAD_DIM), F32)],
        scratch_shapes=[pltpu.VMEM((HB, HEAD_DIM, HEAD_DIM), F32)],
        sem=("arbitrary", "arbitrary"), name="gdn_bwd", carried=carried)
    return dy, dz, da, db, dalog, ddt, dng, moved


def _swa_block(q, k_prev, k_cur, v_prev, v_cur, sink, slope, has_prev):
    kb = jnp.concatenate([k_prev, k_cur], axis=0)
    vb = jnp.concatenate([v_prev, v_cur], axis=0)
    q2 = q.reshape(SWA_GROUP * WINDOW, SWA_DIM)
    s = _mm("nt", SWA_PREC)(q2, kb)
    s = s.reshape(SWA_GROUP, WINDOW, 2 * WINDOW) * (SWA_DIM ** -0.5)
    qi = lax.broadcasted_iota(jnp.int32, (1, WINDOW, 2 * WINDOW), 1)
    sj = lax.broadcasted_iota(jnp.int32, (1, WINDOW, 2 * WINDOW), 2)
    dist = qi + WINDOW - sj
    valid = (dist >= 0) & (dist < WINDOW) & (has_prev | (sj >= WINDOW))
    s = jnp.where(valid, s - slope * dist.astype(F32), -jnp.inf)
    m = lax.stop_gradient(jnp.maximum(jnp.max(s, axis=-1, keepdims=True), sink))
    p = jnp.exp(s - m)
    probs = p / (jnp.sum(p, axis=-1, keepdims=True) + jnp.exp(sink - m))
    o = _mm("nn", SWA_PREC)(probs.reshape(SWA_GROUP * WINDOW, 2 * WINDOW), vb)
    return o.reshape(SWA_GROUP, WINDOW, SWA_DIM)


def _alibi_slopes():
    return (2.0 ** (-8.0 * jnp.arange(1, SWA_HEADS + 1, dtype=F32) / SWA_HEADS)).reshape(SWA_HEADS, 1, 1)


def _swa_fwd(q, k, v, sinks, carried=None):
    T = q.shape[1]
    NB = T // WINDOW

    def body(q_ref, kp_ref, kc_ref, vp_ref, vc_ref, sink_ref, slope_ref, o_ref):
        o = _swa_block(q_ref[...], kp_ref[...], kc_ref[...], vp_ref[...], vc_ref[...], sink_ref[...],
                       slope_ref[...], pl.program_id(1) > 0)
        o_ref[...] = o.astype(BF16)

    qs = pl.BlockSpec((SWA_GROUP, WINDOW, SWA_DIM), lambda h, n: (h, n, 0))
    cur = pl.BlockSpec((None, WINDOW, SWA_DIM), lambda h, n: (h, n, 0))
    prev = pl.BlockSpec((None, WINDOW, SWA_DIM), lambda h, n: (h, jnp.maximum(n - 1, 0), 0))
    hs = pl.BlockSpec((SWA_GROUP, 1, 1), lambda h, n: (h, 0, 0))
    (o,), moved = _call(
        body, (q, k, k, v, v, sinks, _alibi_slopes()), grid=(SWA_KV, NB),
        in_specs=[qs, prev, cur, prev, cur, hs, hs], out_specs=[qs],
        out_shape=[jax.ShapeDtypeStruct((SWA_HEADS, T, SWA_DIM), BF16)],
        sem=("parallel", "parallel"), name="swa_fwd", carried=carried)
    return o, moved


def _swa_bwd(q, k, v, sinks, do, carried=None):
    T = q.shape[1]
    NB = T // WINDOW

    def body(q_ref, kp_ref, kc_ref, vp_ref, vc_ref, sink_ref, slope_ref, do_ref,
             dq_ref, dk_ref, dv_ref, dsink_ref, dk_scr, dv_scr):
        n = pl.program_id(1)

        @pl.when(n == 0)
        def _():
            dk_scr[...] = jnp.zeros_like(dk_scr)
            dv_scr[...] = jnp.zeros_like(dv_scr)
            dsink_ref[...] = jnp.zeros_like(dsink_ref)

        has_prev = n < NB - 1
        fn = functools.partial(_swa_block, slope=slope_ref[...], has_prev=has_prev)
        _, vjp = jax.vjp(fn, q_ref[...], kp_ref[...], kc_ref[...], vp_ref[...], vc_ref[...], sink_ref[...])
        dq, dkp, dkc, dvp, dvc, dsink = vjp(do_ref[...].astype(F32))
        dq_ref[...] = dq.astype(BF16)
        dk_ref[...] = (dkc + dk_scr[...]).astype(BF16)
        dv_ref[...] = (dvc + dv_scr[...]).astype(BF16)
        dk_scr[...] = dkp
        dv_scr[...] = dvp
        dsink_ref[...] += dsink

    rev = lambda n: NB - 1 - n
    qs = pl.BlockSpec((SWA_GROUP, WINDOW, SWA_DIM), lambda h, n: (h, rev(n), 0))
    cur = pl.BlockSpec((None, WINDOW, SWA_DIM), lambda h, n: (h, rev(n), 0))
    prev = pl.BlockSpec((None, WINDOW, SWA_DIM), lambda h, n: (h, jnp.maximum(rev(n) - 1, 0), 0))
    hs = pl.BlockSpec((SWA_GROUP, 1, 1), lambda h, n: (h, 0, 0))
    kv = jax.ShapeDtypeStruct((SWA_KV, T, SWA_DIM), BF16)
    (dq, dk, dv, dsink), moved = _call(
        body, (q, k, k, v, v, sinks, _alibi_slopes(), do), grid=(SWA_KV, NB),
        in_specs=[qs, prev, cur, prev, cur, hs, hs, qs], out_specs=[qs, cur, cur, hs],
        out_shape=[jax.ShapeDtypeStruct((SWA_HEADS, T, SWA_DIM), BF16), kv, kv,
                   jax.ShapeDtypeStruct((SWA_HEADS, 1, 1), F32)],
        scratch_shapes=[pltpu.VMEM((WINDOW, SWA_DIM), F32), pltpu.VMEM((WINDOW, SWA_DIM), F32)],
        sem=("parallel", "arbitrary"), name="swa_bwd", carried=carried)
    return dq, dk, dv, dsink, moved


def _to_heads(t, n_heads):
    return t.reshape(t.shape[0], n_heads, SWA_DIM).transpose(1, 0, 2)


def _from_heads(t):
    return t.transpose(1, 0, 2).reshape(t.shape[1], -1)


def _rows(t):
    return t.T.reshape(t.shape[1], t.shape[0] // CHUNK, 1, CHUNK)


def _unrows(t):
    return t.reshape(t.shape[0], -1).T


def _branch_inputs(proj):
    b_rows = _rows(proj[:, BA0:BA0 + GDN_HEADS])
    a_rows = _rows(proj[:, BA0 + GDN_HEADS:BA0 + 2 * GDN_HEADS])
    q_s = _to_heads(proj[:, QS0:QS0 + D_MODEL], SWA_HEADS)
    k_s = _to_heads(proj[:, KS0:KS0 + KV_W], SWA_KV)
    v_s = _to_heads(proj[:, VS0:VS0 + KV_W], SWA_KV)
    return a_rows, b_rows, q_s, k_s, v_s


def _relu2_epilogue(acc):
    r = jnp.maximum(acc, 0.0)
    return acc, r * r


_BRANCH_OUT = ("w_branch_gdn", "w_branch_swa", "w_out")
_FIRST_NEEDED = ("w_in", "conv_w")


def _gather_plan(l):
    nxt = l + 1
    if l == 0:
        plan = {"mm_proj": [(0, k) for k in _BRANCH_OUT + ("w_ff_up",)], "conv_fwd": [(0, "w_ff_down")],
                "gdn_fwd": [(nxt, k) for k in _FIRST_NEEDED], "swa_fwd": [(nxt, k) for k in _BRANCH_OUT],
                "mm_up": [(nxt, "w_ff_up")], "mm_down": [(nxt, "w_ff_down")]}
    elif nxt < DEPTH:
        plan = {"mm_proj": [(nxt, k) for k in _FIRST_NEEDED], "gdn_fwd": [(nxt, k) for k in _BRANCH_OUT],
                "swa_fwd": [(nxt, "w_ff_up")], "mm_up": [(nxt, "w_ff_down")]}
    else:
        plan = {}
    return plan


def _scatter_plan(l):
    if l == DEPTH - 1:
        return {"swa_bwd": ["w_ff_up"], "gdn_bwd": ["w_ff_down"] + list(_BRANCH_OUT)}
    return {"swa_bwd": ["w_ff_up"], "gdn_bwd": ["late"], "mm_proj_dx": ["w_ff_down"], "mm_proj_dw": list(_BRANCH_OUT)}


class _Weights:
    def __init__(self, params):
        self.params, self.full = params, {}

    def put(self, layer, name, gathered):
        m = _unblock(gathered, _SHARDED[name])
        self.full[layer, name] = _align_w_in(m) if name == "w_in" else m

    def get(self, layer, name):
        return self.full[layer, name] if name in _SHARDED else self.params[name][layer]


def _layer_fwd(x, l, weights, shards):
    plan = _gather_plan(l)
    w = functools.partial(weights.get, l)

    def carry(host):
        items = plan.get(host)
        return _Exchange("gather", [shards[i][k] for i, k in items]) if items else None

    def keep(host, moved):
        for (i, k), gathered in zip(plan.get(host, ()), moved or ()):
            weights.put(i, k, gathered)

    h1 = _rms_fwd(x, w("norm1_g"), "rms1_fwd")
    proj, *moved = _matmul(h1, w("w_in"), "nn", [F32], name="mm_proj", carried=carry("mm_proj"))
    keep("mm_proj", moved[0] if moved else None)
    a_rows, b_rows, q_s, k_s, v_s = _branch_inputs(proj)
    y, moved = _conv_fwd(proj, w("conv_w"), carried=carry("conv_fwd"))
    keep("conv_fwd", moved)
    o_gdn, states, moved = _gdn_fwd(y, proj, a_rows, b_rows, w("a_log").reshape(-1, 1, 1),
                                    w("dt_bias").reshape(-1, 1, 1), w("gdn_norm_g").reshape(1, 1, -1),
                                    carried=carry("gdn_fwd"))
    keep("gdn_fwd", moved)
    o_swa, moved = _swa_fwd(q_s, k_s, v_s, w("attn_sinks").reshape(-1, 1, 1), carried=carry("swa_fwd"))
    keep("swa_fwd", moved)
    o_swa = _from_heads(o_swa)
    y_gdn, = _matmul(o_gdn, w("w_branch_gdn"), "nn", [F32], name="mm_bgdn")
    y_swa, = _matmul(o_swa, w("w_branch_swa"), "nn", [F32], name="mm_bswa")
    mix = _gate_mix_fwd(proj, y_gdn, y_swa)
    x2, = _matmul(mix, w("w_out"), "nn", [F32], epilogue=lambda acc, r: (r + acc,), extras=(x,), name="mm_out")
    h2 = _rms_fwd(x2, w("norm2_g"), "rms2_fwd")
    u, act, *moved = _matmul(h2, w("w_ff_up"), "nn", [F32, BF16], epilogue=_relu2_epilogue, name="mm_up",
                             carried=carry("mm_up"))
    keep("mm_up", moved[0] if moved else None)
    x3, *moved = _matmul(act, w("w_ff_down"), "nn", [F32], epilogue=lambda acc, r: (r + acc,), extras=(x2,),
                         name="mm_down", carried=carry("mm_down"))
    keep("mm_down", moved[0] if moved else None)
    saved = dict(x=x, h1=h1, proj=proj, y=y, states=states, o_gdn=o_gdn, o_swa=o_swa, y_gdn=y_gdn, y_swa=y_swa,
                 mix=mix, x2=x2, h2=h2, u=u, act=act)
    return x3, saved


def _layer_bwd(dx3, l, weights, s, late, landed):
    T = dx3.shape[0]
    g = {}
    plan = _scatter_plan(l)
    w = functools.partial(weights.get, l)

    def carry(host):
        arrays = []
        for k in plan.get(host, ()):
            arrays += late if k == "late" else [_block(g[k], _SHARDED[k])]
        return _Exchange("scatter", arrays) if arrays else None

    def keep(host, moved):
        names = []
        for k in plan.get(host, ()):
            names += [(l + 1, n) for n in _FIRST_NEEDED] if k == "late" else [(l, k)]
        for (i, k), blocks in zip(names, moved or ()):
            landed[i][k] = blocks

    dx3b = dx3.astype(BF16)
    du, = _matmul(dx3b, w("w_ff_down"), "nt", [BF16], extras=(s["u"],),
                  epilogue=lambda acc, uu: (acc * 2.0 * jnp.maximum(uu, 0.0),), name="mm_down_dx")
    g["w_ff_down"], = _matmul(s["act"], dx3b, "tn", [BF16], name="mm_down_dw")
    dh2, = _matmul(du, w("w_ff_up"), "nt", [F32], name="mm_up_dx")
    g["w_ff_up"], = _matmul(s["h2"], du, "tn", [BF16], name="mm_up_dw")
    dx2, g["norm2_g"] = _rms_bwd(dh2, s["x2"], w("norm2_g"), dx3, "rms2_bwd")
    dx2b = dx2.astype(BF16)
    dmix, = _matmul(dx2b, w("w_out"), "nt", [F32], name="mm_out_dx")
    g["w_out"], = _matmul(s["mix"], dx2b, "tn", [BF16], name="mm_out_dw")
    dy_gdn, dy_swa, dgg, dgs = _gate_mix_bwd(dmix, s["proj"], s["y_gdn"], s["y_swa"])
    do_gdn, = _matmul(dy_gdn, w("w_branch_gdn"), "nt", [BF16], name="mm_bgdn_dx")
    g["w_branch_gdn"], = _matmul(s["o_gdn"], dy_gdn, "tn", [BF16], name="mm_bgdn_dw")
    do_swa, = _matmul(dy_swa, w("w_branch_swa"), "nt", [BF16], name="mm_bswa_dx")
    g["w_branch_swa"], = _matmul(s["o_swa"], dy_swa, "tn", [BF16], name="mm_bswa_dw")
    a_rows, b_rows, q_s, k_s, v_s = _branch_inputs(s["proj"])
    dq_s, dk_s, dv_s, dsink, moved = _swa_bwd(q_s, k_s, v_s, w("attn_sinks").reshape(-1, 1, 1),
                                              _to_heads(do_swa, SWA_HEADS), carried=carry("swa_bwd"))
    keep("swa_bwd", moved)
    g["attn_sinks"] = dsink.reshape(-1)
    dy, dz, da_rows, db_rows, dalog, ddt, dng, moved = _gdn_bwd(
        s["y"], s["proj"], a_rows, b_rows, w("a_log").reshape(-1, 1, 1), w("dt_bias").reshape(-1, 1, 1),
        w("gdn_norm_g").reshape(1, 1, -1), s["states"], do_gdn, carried=carry("gdn_bwd"))
    keep("gdn_bwd", moved)
    g["a_log"], g["dt_bias"], g["gdn_norm_g"] = dalog.reshape(-1), ddt.reshape(-1), dng.reshape(-1)
    dqkv, g["conv_w"], _ = _conv_bwd(dy, s["proj"], w("conv_w"))
    dba = jnp.concatenate([_unrows(db_rows), _unrows(da_rows)], axis=1).astype(BF16)
    dproj = jnp.concatenate(
        [dqkv, dz, dgg, dgs, _from_heads(dq_s), _from_heads(dk_s), _from_heads(dv_s), dba,
         jnp.zeros((T, N_PROJ - BA0 - 2 * GDN_HEADS), BF16)], axis=1)
    dh1, *moved = _matmul(dproj, w("w_in"), "nt", [F32], name="mm_proj_dx", carried=carry("mm_proj_dx"))
    keep("mm_proj_dx", moved[0] if moved else None)
    g["w_in"], *moved = _matmul(s["h1"], dproj, "tn", [BF16], name="mm_proj_dw", carried=carry("mm_proj_dw"))
    keep("mm_proj_dw", moved[0] if moved else None)
    dx, g["norm1_g"] = _rms_bwd(dh1, s["x"], w("norm1_g"), dx2, "rms1_bwd")
    g["norm1_g"], g["norm2_g"] = g["norm1_g"].reshape(-1), g["norm2_g"].reshape(-1)
    own_late = [_block(_unalign_w_in(g["w_in"]), _SHARDED["w_in"]), _block(g["conv_w"], _SHARDED["conv_w"]).astype(BF16)]
    return dx, g, own_late


def _all_reduce_small(part):
    R = part.shape[0]

    def body(x_ref, sum_ref, gath, send_sems, recv_sems):
        me = _my_index()
        gath[me] = x_ref[...]
        sends = []
        for mask in range(1, N_DEV):
            dev, _ = _peer(mask)
            sends.append(pltpu.make_async_remote_copy(
                src_ref=x_ref, dst_ref=gath.at[me], send_sem=send_sems.at[mask - 1], recv_sem=recv_sems.at[mask - 1],
                device_id=dev, device_id_type=MESH))
        for cp in sends:
            cp.start()
        for mask in range(1, N_DEV):
            dev, idx = _peer(mask)
            pltpu.make_async_remote_copy(
                src_ref=x_ref, dst_ref=gath.at[idx], send_sem=send_sems.at[mask - 1], recv_sem=recv_sems.at[mask - 1],
                device_id=dev, device_id_type=MESH).wait_recv()
        for cp in sends:
            cp.wait_send()
        acc = gath[0]
        for i in range(1, N_DEV):
            acc = acc + gath[i]
        sum_ref[...] = acc

    vm = pl.BlockSpec(memory_space=pltpu.VMEM)
    return pl.pallas_call(
        body, in_specs=[vm], out_specs=vm, out_shape=jax.ShapeDtypeStruct((R, 128), F32),
        scratch_shapes=[pltpu.VMEM((N_DEV, R, 128), F32), pltpu.SemaphoreType.DMA((7,)), pltpu.SemaphoreType.DMA((7,))],
        name="small_all_reduce",
    )(part)


def _adam_math(w, g, m, v):
    m = ADAM_B1 * m + (1.0 - ADAM_B1) * g
    v = ADAM_B2 * v + (1.0 - ADAM_B2) * (g * g)
    m_hat = m / (1.0 - ADAM_B1 ** ADAM_STEP)
    v_hat = v / (1.0 - ADAM_B2 ** ADAM_STEP)
    delta = -ADAM_LR * (m_hat / (jnp.sqrt(v_hat) + ADAM_EPS) + ADAM_WD * w)
    return delta, m, v


def _adamw_sum(parts, w, m, v, name):
    n_layers = len(parts)
    R, Cc = parts[0].shape[1:]
    tr = R
    while tr * Cc * 4 > (1 << 20) and tr % 32 == 0:
        tr //= 2
    nblk = R // tr

    def body(*refs):
        p_refs = refs[:n_layers]
        w_ref, m_ref, v_ref, g_ref, d_ref, nm_ref, nv_ref = refs[n_layers:]
        for j in range(n_layers):
            @pl.when(pl.program_id(0) == j)
            def _(p_ref=p_refs[j]):
                g = p_ref[0].astype(F32)
                for i in range(1, N_DEV):
                    g = g + p_ref[i].astype(F32)
                d, nm, nv = _adam_math(w_ref[...], g, m_ref[...], v_ref[...])
                g_ref[...], d_ref[...], nm_ref[...], nv_ref[...] = g, d, nm, nv

    def part_spec(j):
        return pl.BlockSpec((N_DEV, tr, Cc), lambda l, i: (0, jnp.where(l == j, i, jnp.where(l < j, 0, nblk - 1)), 0))

    blk = pl.BlockSpec((tr, Cc), lambda l, i: (l * nblk + i, 0))
    out = jax.ShapeDtypeStruct((n_layers * R, Cc), F32)
    return pl.pallas_call(
        body, grid=(n_layers, nblk), in_specs=[part_spec(j) for j in range(n_layers)] + [blk, blk, blk],
        out_specs=[blk] * 4, out_shape=[out] * 4, compiler_params=_cp("arbitrary", "arbitrary"), name=name,
    )(*parts, w, m, v)


def _adamw_small(g, w, m, v):
    def body(g_ref, w_ref, m_ref, v_ref, d_ref, nm_ref, nv_ref):
        d_ref[...], nm_ref[...], nv_ref[...] = _adam_math(w_ref[...], g_ref[...], m_ref[...], v_ref[...])

    out = jax.ShapeDtypeStruct(g.shape, F32)
    return pl.pallas_call(body, out_shape=[out] * 3, name="adamw_small")(g, w, m, v)


def _align_w_in(w_full):
    by_dst = sorted(_SEGS, key=lambda s: s[2])
    parts = [w_full[:, src:src + width] for src, width, _ in by_dst]
    end = by_dst[-1][2] + by_dst[-1][1]
    return jnp.concatenate(parts + [jnp.zeros((w_full.shape[0], N_PROJ - end), w_full.dtype)], axis=1)


def _unalign_w_in(g):
    return jnp.concatenate([g[:, dst:dst + width] for _, width, dst in _SEGS], axis=1)


_SHARDED = {"w_in": 1, "conv_w": 1, "w_branch_gdn": 0, "w_branch_swa": 0, "w_out": 0, "w_ff_up": 1, "w_ff_down": 0}
_SMALL = ("norm1_g", "a_log", "dt_bias", "gdn_norm_g", "attn_sinks", "norm2_g", "final_norm_g")


def _unblock(gathered, axis):
    if axis == 0:
        return gathered.reshape(-1, gathered.shape[2])
    return gathered.transpose(1, 0, 2).reshape(gathered.shape[1], -1)


def _block(full, axis):
    A, B = full.shape
    if axis == 0:
        return full.reshape(N_DEV, A // N_DEV, B)
    return full.reshape(A, N_DEV, B // N_DEV).transpose(1, 0, 2)


def _pack_small(d, loss_row=None):
    rows = [d[k].astype(F32).reshape(-1, 128) if d[k].size % 128 == 0 else
            jnp.pad(d[k].astype(F32), ((0, 0), (0, 128 - d[k].shape[-1]))) for k in _SMALL]
    rows.append(jnp.zeros((1, 128), F32) if loss_row is None else loss_row)
    packed = jnp.concatenate(rows, axis=0)
    return jnp.pad(packed, ((0, -packed.shape[0] % 8), (0, 0)))


def _unpack_small(packed, like):
    out, r = {}, 0
    for k in _SMALL:
        shp = like[k].shape
        if like[k].size % 128 == 0:
            n = like[k].size // 128
            out[k] = packed[r:r + n].reshape(shp)
        else:
            n = shp[0]
            out[k] = packed[r:r + n, :shp[-1]]
        r += n
    return out, packed[r, 0]


def kernel(x, norm1_g, w_in, conv_w, a_log, dt_bias, gdn_norm_g, attn_sinks, w_branch_gdn, w_branch_swa, w_out, norm2_g, w_ff_up, w_ff_down, final_norm_g, loss_target, m_norm1_g, m_w_in, m_conv_w, m_a_log, m_dt_bias, m_gdn_norm_g, m_attn_sinks, m_w_branch_gdn, m_w_branch_swa, m_w_out, m_norm2_g, m_w_ff_up, m_w_ff_down, m_final_norm_g, v_norm1_g, v_w_in, v_conv_w, v_a_log, v_dt_bias, v_gdn_norm_g, v_attn_sinks, v_w_branch_gdn, v_w_branch_swa, v_w_out, v_norm2_g, v_w_ff_up, v_w_ff_down, v_final_norm_g):
    names = ("norm1_g", "w_in", "conv_w", "a_log", "dt_bias", "gdn_norm_g", "attn_sinks", "w_branch_gdn",
             "w_branch_swa", "w_out", "norm2_g", "w_ff_up", "w_ff_down", "final_norm_g")
    w = dict(zip(names, (norm1_g, w_in, conv_w, a_log, dt_bias, gdn_norm_g, attn_sinks, w_branch_gdn, w_branch_swa,
                         w_out, norm2_g, w_ff_up, w_ff_down, final_norm_g)))
    m = dict(zip(names, (m_norm1_g, m_w_in, m_conv_w, m_a_log, m_dt_bias, m_gdn_norm_g, m_attn_sinks, m_w_branch_gdn,
                         m_w_branch_swa, m_w_out, m_norm2_g, m_w_ff_up, m_w_ff_down, m_final_norm_g)))
    v = dict(zip(names, (v_norm1_g, v_w_in, v_conv_w, v_a_log, v_dt_bias, v_gdn_norm_g, v_attn_sinks, v_w_branch_gdn,
                         v_w_branch_swa, v_w_out, v_norm2_g, v_w_ff_up, v_w_ff_down, v_final_norm_g)))
    sharded = list(_SHARDED)

    shards = [{k: w[k][l] if k == "conv_w" else w[k][l].astype(BF16) for k in sharded} for l in range(DEPTH)]

    weights = _Weights(w)
    for k, gathered in zip(_FIRST_NEEDED, _exchange("gather", [shards[0][k] for k in _FIRST_NEEDED], "weight_all_gather")):
        weights.put(0, k, gathered)
    xc, saved = x[0], []
    for l in range(DEPTH):
        xc, s = _layer_fwd(xc, l, weights, shards)
        saved.append(s)
    loss_row, dx, dgf = _loss_head(xc, final_norm_g, loss_target[0])

    grads, landed, late = [None] * DEPTH, [{} for _ in range(DEPTH)], None
    for l in reversed(range(DEPTH)):
        dx, grads[l], late = _layer_bwd(dx, l, weights, saved[l], late, landed)
    landed[0].update(zip(_FIRST_NEEDED, _exchange("scatter", late, "grad_scatter")))
    grad_x = dx
    out_g, out_d, out_m, out_v = {}, {}, {}, {}
    for k in sharded:
        shp = w[k].shape
        flat = lambda t: t.reshape(-1, shp[-1])
        parts = [landed[l][k] for l in range(DEPTH)]
        if parts[0].shape[1] % 8:
            parts = [jnp.concatenate(parts, axis=1)]
        res = _adamw_sum(parts, flat(w[k]), flat(m[k]), flat(v[k]), "adamw_" + k)
        out_g[k], out_d[k], out_m[k], out_v[k] = [t.reshape(shp) for t in res]

    small_g = {k: jnp.stack([gl[k] for gl in grads]) for k in _SMALL if k != "final_norm_g"}
    small_g["final_norm_g"] = dgf.reshape(-1)
    total = _all_reduce_small(_pack_small(small_g, loss_row))
    sd, sm, sv = _adamw_small(total, _pack_small(w), _pack_small(m), _pack_small(v))
    g_small, loss = _unpack_small(total, w)
    d_small, _ = _unpack_small(sd, w)
    m_small, _ = _unpack_small(sm, w)
    v_small, _ = _unpack_small(sv, w)
    out_g.update(g_small), out_d.update(d_small), out_m.update(m_small), out_v.update(v_small)

    return (loss, grad_x[None], *[out_g[k] for k in names], *[out_d[k] for k in names],
            *[out_m[k] for k in names], *[out_v[k] for k in names])
```

```python
import functools

import jax
import jax.numpy as jnp
from jax import lax
from jax.experimental import pallas as pl
from jax.experimental.pallas import tpu as pltpu

F32 = jnp.float32
BF16 = jnp.bfloat16
HI = lax.Precision.HIGHEST
MESH = pl.DeviceIdType.MESH

N_DEV = 8
D_MODEL = 2048
DEPTH = 4
GDN_HEADS = 16
HEAD_DIM = 128
CHUNK = 64
CONV_K = 4
SWA_HEADS = 32
SWA_KV = 4
SWA_GROUP = SWA_HEADS // SWA_KV
SWA_DIM = 64
WINDOW = 128
NORM_EPS = 1e-6
GDN_W = GDN_HEADS * HEAD_DIM
KV_W = SWA_KV * SWA_DIM

QKV0, Z0, GG0, GS0, QS0, KS0, VS0, BA0 = 0, 6144, 8192, 10240, 12288, 14336, 14592, 14848
N_PROJ = 15360
_SEGS = ((0, 6144, QKV0), (6144, 2048, Z0), (8192, 32, BA0), (8224, 2048, QS0), (10272, 256, KS0),
         (10528, 256, VS0), (10784, 4096, GG0))

ADAM_LR, ADAM_B1, ADAM_B2, ADAM_EPS, ADAM_WD, ADAM_STEP = 0.001, 0.9, 0.999, 1e-08, 0.01, 10

VMEM_LIMIT = 56 * 1024 * 1024
GDN_HB = 16
GDN_PREC_SOLVE = "x3"
GDN_PREC = "bf16"
SWA_PREC = "bf16"
SWA_KV_STEP = 2


def _cp(*sem):
    return pltpu.CompilerParams(dimension_semantics=sem, vmem_limit_bytes=VMEM_LIMIT)


def _tile(n, want):
    t = min(n, want)
    while n % t:
        t -= 128
    return t


def _my_index():
    return 4 * lax.axis_index("x") + 2 * lax.axis_index("y") + lax.axis_index("c")


def _peer(mask):
    x, y, c = lax.axis_index("x"), lax.axis_index("y"), lax.axis_index("c")
    px = 1 - x if mask & 4 else x
    py = 1 - y if mask & 2 else y
    pc = 1 - c if mask & 1 else c
    return (px, py, pc), 4 * px + 2 * py + pc


class _Exchange:
    def __init__(self, kind, arrays):
        self.kind, self.arrays, self.n = kind, list(arrays), len(arrays)
        self.out_shape = [jax.ShapeDtypeStruct(((N_DEV,) + a.shape) if kind == "gather" else a.shape, a.dtype)
                          for a in self.arrays]
        self.scratch = [pltpu.SemaphoreType.DMA((self.n, 7)), pltpu.SemaphoreType.DMA((self.n, 7)),
                        pltpu.SemaphoreType.DMA((self.n,))]

    def _local(self, ins, outs, sems):
        me = _my_index()
        src = (lambda a: ins[a]) if self.kind == "gather" else (lambda a: ins[a].at[me])
        return [pltpu.make_async_copy(src(a), outs[a].at[me], sems[2].at[a]) for a in range(self.n)]

    def _copy(self, outs, sems, a, k, src, block, to):
        return pltpu.make_async_remote_copy(src_ref=src, dst_ref=outs[a].at[block], send_sem=sems[0].at[a, k],
                                            recv_sem=sems[1].at[a, k], device_id=to, device_id_type=MESH)

    def _sends(self, ins, outs, sems):
        me = _my_index()
        cps = []
        for a in range(self.n):
            if self.kind == "gather":
                for k, mask in enumerate((1, 4, 2, 6)):
                    cps.append(self._copy(outs, sems, a, k, ins[a], me, _peer(mask)[0]))
            else:
                for mask in range(1, N_DEV):
                    dev, idx = _peer(mask)
                    cps.append(self._copy(outs, sems, a, mask - 1, ins[a].at[idx], me, dev))
        return cps

    def _relays(self, outs, sems):
        sibling = _peer(1)[0]
        return [self._copy(outs, sems, a, 4 + j, outs[a].at[_peer(mask)[1]], _peer(mask)[1], sibling)
                for a in range(self.n) for j, mask in enumerate((4, 2, 6))]

    def start(self, ins, outs, sems):
        for cp in self._local(ins, outs, sems) + self._sends(ins, outs, sems):
            cp.start()

    def finish(self, ins, outs, sems):
        me = _my_index()
        if self.kind == "gather":
            relays = self._relays(outs, sems)
            for a in range(self.n):
                for j, mask in enumerate((4, 2, 6)):
                    self._copy(outs, sems, a, 1 + j, ins[a], _peer(mask)[1], _peer(mask)[0]).wait_recv()
                    relays[3 * a + j].start()
            for a in range(self.n):
                self._copy(outs, sems, a, 0, ins[a], _peer(1)[1], _peer(1)[0]).wait_recv()
                for j, mask in enumerate((4, 2, 6)):
                    self._copy(outs, sems, a, 4 + j, ins[a], _peer(mask)[1] ^ 1, _peer(1)[0]).wait_recv()
            for cp in relays:
                cp.wait_send()
        else:
            for a in range(self.n):
                for mask in range(1, N_DEV):
                    dev, idx = _peer(mask)
                    self._copy(outs, sems, a, mask - 1, ins[a].at[idx], idx, dev).wait_recv()
        for cp in self._sends(ins, outs, sems):
            cp.wait_send()
        for cp in self._local(ins, outs, sems):
            cp.wait()


def _exchange(kind, arrays, name):
    ex = _Exchange(kind, arrays)

    def body(*refs):
        ins, outs, sems = refs[:ex.n], refs[ex.n:2 * ex.n], refs[2 * ex.n:]
        ex.start(ins, outs, sems)
        ex.finish(ins, outs, sems)

    any_spec = pl.BlockSpec(memory_space=pl.ANY)
    return pl.pallas_call(body, in_specs=[any_spec] * ex.n, out_specs=[any_spec] * ex.n, out_shape=ex.out_shape,
                          scratch_shapes=ex.scratch, name=name)(*ex.arrays)


def _call(body, args, *, grid, in_specs, out_specs, out_shape, scratch_shapes=(), sem, name, carried=None):
    if carried is None:
        outs = pl.pallas_call(body, grid=grid, in_specs=list(in_specs), out_specs=list(out_specs),
                              out_shape=list(out_shape), scratch_shapes=list(scratch_shapes),
                              compiler_params=_cp(*sem), name=name)(*args)
        return outs, None
    n_in, n_out, n_scr, n_c = len(args), len(out_shape), len(scratch_shapes), carried.n

    def wrapped(*refs):
        ins, c_in = refs[:n_in], refs[n_in:n_in + n_c]
        o0 = n_in + n_c
        outs, c_out = refs[o0:o0 + n_out], refs[o0 + n_out:o0 + n_out + n_c]
        s0 = o0 + n_out + n_c
        scr, c_sems = refs[s0:s0 + n_scr], refs[s0 + n_scr:]
        ids = [pl.program_id(i) for i in range(len(grid))]
        first = functools.reduce(jnp.logical_and, [i == 0 for i in ids])
        last = functools.reduce(jnp.logical_and, [i == g - 1 for i, g in zip(ids, grid)])

        @pl.when(first)
        def _():
            carried.start(c_in, c_out, c_sems)

        body(*ins, *outs, *scr)

        @pl.when(last)
        def _():
            carried.finish(c_in, c_out, c_sems)

    any_spec = pl.BlockSpec(memory_space=pl.ANY)
    res = pl.pallas_call(
        wrapped, grid=grid, in_specs=list(in_specs) + [any_spec] * n_c, out_specs=list(out_specs) + [any_spec] * n_c,
        out_shape=list(out_shape) + carried.out_shape, scratch_shapes=list(scratch_shapes) + carried.scratch,
        compiler_params=_cp(*["arbitrary"] * len(grid)), name=name)(*args, *carried.arrays)
    return res[:n_out], res[n_out:]


def _matmul(a, b, mode, out_shapes, epilogue=None, extras=(), name="mm", carried=None):
    if mode == "tn":
        K, M = a.shape
    else:
        M, K = a.shape
    N = b.shape[0] if mode == "nt" else b.shape[1]
    tm, tn, tk = _tile(M, 1024), _tile(N, 1024), _tile(K, 2048)
    nk = K // tk
    dims = {"nn": (((1,), (0,)), ((), ())), "nt": (((1,), (1,)), ((), ())), "tn": (((0,), (0,)), ((), ()))}[mode]
    n_ex, n_out = len(extras), len(out_shapes)

    def body(*refs):
        a_ref, b_ref = refs[:2]
        ex = refs[2:2 + n_ex]
        outs = refs[2 + n_ex:2 + n_ex + n_out]
        k = pl.program_id(2)

        def product():
            return lax.dot_general(a_ref[...].astype(BF16), b_ref[...].astype(BF16), dims,
                                   preferred_element_type=F32)

        def finish(total):
            res = (total,) if epilogue is None else epilogue(total, *[e[...] for e in ex])
            for o, r in zip(outs, res):
                o[...] = r.astype(o.dtype)

        if nk == 1:
            finish(product())
            return
        acc = refs[-1]

        @pl.when(k == 0)
        def _():
            acc[...] = product()

        @pl.when((k > 0) & (k < nk - 1))
        def _():
            acc[...] += product()

        @pl.when(k == nk - 1)
        def _():
            finish(acc[...] + product())

    a_spec = pl.BlockSpec((tk, tm), lambda i, j, k: (k, i)) if mode == "tn" else pl.BlockSpec((tm, tk), lambda i, j, k: (i, k))
    b_spec = pl.BlockSpec((tn, tk), lambda i, j, k: (j, k)) if mode == "nt" else pl.BlockSpec((tk, tn), lambda i, j, k: (k, j))
    mn = pl.BlockSpec((tm, tn), lambda i, j, k: (i, j))
    outs, moved = _call(
        body, (a, b, *extras), grid=(M // tm, N // tn, nk),
        in_specs=[a_spec, b_spec] + [mn] * n_ex, out_specs=[mn] * n_out,
        out_shape=[jax.ShapeDtypeStruct((M, N), dt) for dt in out_shapes],
        scratch_shapes=[pltpu.VMEM((tm, tn), F32)] if nk > 1 else [],
        sem=("parallel", "parallel", "arbitrary"), name=name, carried=carried)
    return list(outs) if carried is None else list(outs) + [moved]


def _rms_fwd(x, g, name):
    T, D = x.shape
    tr = _tile(T, 256)

    def body(x_ref, g_ref, h_ref):
        xv = x_ref[...]
        r = lax.rsqrt(jnp.mean(xv * xv, axis=-1, keepdims=True) + NORM_EPS)
        h_ref[...] = (xv * r * g_ref[...]).astype(BF16)

    return pl.pallas_call(
        body, grid=(T // tr,),
        in_specs=[pl.BlockSpec((tr, D), lambda i: (i, 0)), pl.BlockSpec((1, D), lambda i: (0, 0))],
        out_specs=pl.BlockSpec((tr, D), lambda i: (i, 0)),
        out_shape=jax.ShapeDtypeStruct((T, D), BF16), compiler_params=_cp("parallel"), name=name,
    )(x, g.reshape(1, D))


def _rms_bwd(dh, x, g, dres, name):
    T, D = x.shape
    tr = _tile(T, 256)

    def body(dh_ref, x_ref, g_ref, dres_ref, dx_ref, dxb_ref, dg_ref):
        @pl.when(pl.program_id(0) == 0)
        def _():
            dg_ref[...] = jnp.zeros_like(dg_ref)

        xv = x_ref[...]
        r = lax.rsqrt(jnp.mean(xv * xv, axis=-1, keepdims=True) + NORM_EPS)
        xhat = xv * r
        dhv = dh_ref[...].astype(F32)
        gd = dhv * g_ref[...]
        dx = dres_ref[...] + r * (gd - xhat * jnp.mean(gd * xhat, axis=-1, keepdims=True))
        dx_ref[...] = dx
        dxb_ref[...] = dx.astype(BF16)
        dg_ref[...] += jnp.sum(dhv * xhat, axis=0, keepdims=True)

    row = pl.BlockSpec((tr, D), lambda i: (i, 0))
    vec = pl.BlockSpec((1, D), lambda i: (0, 0))
    return pl.pallas_call(
        body, grid=(T // tr,), in_specs=[row, row, vec, row], out_specs=[row, row, vec],
        out_shape=[jax.ShapeDtypeStruct((T, D), F32), jax.ShapeDtypeStruct((T, D), BF16),
                   jax.ShapeDtypeStruct((1, D), F32)],
        compiler_params=_cp("arbitrary"), name=name,
    )(dh, x, g.reshape(1, D), dres)


def _loss_head(x, g, tgt):
    T, D = x.shape
    tr = _tile(T, 256)

    def body(x_ref, g_ref, t_ref, loss_ref, dx_ref, dxb_ref, dg_ref):
        @pl.when(pl.program_id(0) == 0)
        def _():
            dg_ref[...] = jnp.zeros_like(dg_ref)
            loss_ref[...] = jnp.zeros_like(loss_ref)

        xv = x_ref[...]
        r = lax.rsqrt(jnp.mean(xv * xv, axis=-1, keepdims=True) + NORM_EPS)
        xhat = xv * r
        err = xhat * g_ref[...] - t_ref[...]
        loss_ref[...] += (0.5 / D) * jnp.sum(jnp.sum(err * err, axis=-1, keepdims=True), axis=0, keepdims=True)
        dy = err * (1.0 / D)
        gd = dy * g_ref[...]
        dx = r * (gd - xhat * jnp.mean(gd * xhat, axis=-1, keepdims=True))
        dx_ref[...] = dx
        dxb_ref[...] = dx.astype(BF16)
        dg_ref[...] += jnp.sum(dy * xhat, axis=0, keepdims=True)

    row = pl.BlockSpec((tr, D), lambda i: (i, 0))
    vec = pl.BlockSpec((1, D), lambda i: (0, 0))
    return pl.pallas_call(
        body, grid=(T // tr,), in_specs=[row, vec, row],
        out_specs=[pl.BlockSpec((1, 128), lambda i: (0, 0)), row, row, vec],
        out_shape=[jax.ShapeDtypeStruct((1, 128), F32), jax.ShapeDtypeStruct((T, D), F32),
                   jax.ShapeDtypeStruct((T, D), BF16), jax.ShapeDtypeStruct((1, D), F32)],
        compiler_params=_cp("arbitrary"), name="loss_head",
    )(x, g.reshape(1, D), tgt)


def _sigmoid(x):
    return 1.0 / (1.0 + jnp.exp(-x))


def _gate_mix_fwd(proj, y_gdn, y_swa):
    T = proj.shape[0]
    tr, tc = _tile(T, 512), 512
    nc = D_MODEL // tc

    def body(gg_ref, gs_ref, yg_ref, ys_ref, mix_ref):
        mix_ref[...] = (_sigmoid(gg_ref[...]) * yg_ref[...] + _sigmoid(gs_ref[...]) * ys_ref[...]).astype(BF16)

    blk = pl.BlockSpec((tr, tc), lambda i, j: (i, j))
    return pl.pallas_call(
        body, grid=(T // tr, nc),
        in_specs=[pl.BlockSpec((tr, tc), lambda i, j: (i, GG0 // tc + j)),
                  pl.BlockSpec((tr, tc), lambda i, j: (i, GS0 // tc + j)), blk, blk],
        out_specs=blk, out_shape=jax.ShapeDtypeStruct((T, D_MODEL), BF16),
        compiler_params=_cp("parallel", "parallel"), name="gate_mix_fwd",
    )(proj, proj, y_gdn, y_swa)


def _gate_mix_bwd(dmix, proj, y_gdn, y_swa):
    T = proj.shape[0]
    tr, tc = _tile(T, 512), 512
    nc = D_MODEL // tc

    def body(dm_ref, gg_ref, gs_ref, yg_ref, ys_ref, dyg_ref, dys_ref, dgg_ref, dgs_ref):
        dm = dm_ref[...]
        sg, ss = _sigmoid(gg_ref[...]), _sigmoid(gs_ref[...])
        dyg_ref[...] = (dm * sg).astype(BF16)
        dys_ref[...] = (dm * ss).astype(BF16)
        dgg_ref[...] = (dm * yg_ref[...] * sg * (1.0 - sg)).astype(BF16)
        dgs_ref[...] = (dm * ys_ref[...] * ss * (1.0 - ss)).astype(BF16)

    blk = pl.BlockSpec((tr, tc), lambda i, j: (i, j))
    out = jax.ShapeDtypeStruct((T, D_MODEL), BF16)
    return pl.pallas_call(
        body, grid=(T // tr, nc),
        in_specs=[blk, pl.BlockSpec((tr, tc), lambda i, j: (i, GG0 // tc + j)),
                  pl.BlockSpec((tr, tc), lambda i, j: (i, GS0 // tc + j)), blk, blk],
        out_specs=[blk] * 4, out_shape=[out] * 4,
        compiler_params=_cp("parallel", "parallel"), name="gate_mix_bwd",
    )(dmix, proj, proj, y_gdn, y_swa)


CONV_TT, CONV_TW, HALO = 512, 256, 8


def _conv_fwd(proj, conv_w, carried=None):
    T = proj.shape[0]
    tt = _tile(T, CONV_TT)
    W = 3 * GDN_W

    def body(x_ref, halo_ref, w_ref, y_ref):
        first = pl.program_id(1) == 0
        halo = jnp.where(first, 0.0, halo_ref[...])
        xe = jnp.concatenate([halo, x_ref[...]], axis=0)
        acc = xe[HALO:] * w_ref[pl.ds(CONV_K - 1, 1), :]
        for j in range(CONV_K - 1):
            acc = acc + pltpu.roll(xe, CONV_K - 1 - j, 0)[HALO:] * w_ref[pl.ds(j, 1), :]
        y_ref[...] = acc

    (y,), moved = _call(
        body, (proj, proj, conv_w), grid=(W // CONV_TW, T // tt),
        in_specs=[pl.BlockSpec((tt, CONV_TW), lambda c, t: (t, c)),
                  pl.BlockSpec((HALO, CONV_TW), lambda c, t: (jnp.maximum(t * (tt // HALO) - 1, 0), c)),
                  pl.BlockSpec((CONV_K, CONV_TW), lambda c, t: (0, c))],
        out_specs=[pl.BlockSpec((tt, CONV_TW), lambda c, t: (t, c))],
        out_shape=[jax.ShapeDtypeStruct((T, W), F32)],
        sem=("parallel", "parallel"), name="conv_fwd", carried=carried)
    return y, moved


def _conv_bwd(dy, proj, conv_w, carried=None):
    T = proj.shape[0]
    tt = _tile(T, CONV_TT)
    nt = T // tt
    W = 3 * GDN_W

    def body(dy_ref, dnext_ref, x_ref, halo_ref, w_ref, dx_ref, dw_ref):
        t = pl.program_id(1)

        @pl.when(t == 0)
        def _():
            dw_ref[...] = jnp.zeros_like(dw_ref)

        dyv = dy_ref[...]
        dye = jnp.concatenate([dyv, jnp.where(t == nt - 1, 0.0, dnext_ref[...])], axis=0)
        xe = jnp.concatenate([jnp.where(t == 0, 0.0, halo_ref[...]), x_ref[...]], axis=0)
        acc = dyv * w_ref[pl.ds(CONV_K - 1, 1), :]
        dw_ref[pl.ds(CONV_K - 1, 1), :] += jnp.sum(dyv * xe[HALO:], axis=0, keepdims=True)
        for s in range(1, CONV_K):
            j = CONV_K - 1 - s
            acc = acc + pltpu.roll(dye, tt + HALO - s, 0)[:tt] * w_ref[pl.ds(j, 1), :]
            dw_ref[pl.ds(j, 1), :] += jnp.sum(dyv * pltpu.roll(xe, s, 0)[HALO:], axis=0, keepdims=True)
        dx_ref[...] = acc.astype(BF16)

    cur = pl.BlockSpec((tt, CONV_TW), lambda c, t: (t, c))
    per_plane = GDN_W // CONV_TW
    (dx, dw), moved = _call(
        body, (dy, dy, proj, proj, conv_w), grid=(W // CONV_TW, nt),
        in_specs=[pl.BlockSpec((None, tt, CONV_TW), lambda c, t: (c // per_plane, t, c % per_plane)),
                  pl.BlockSpec((None, HALO, CONV_TW),
                               lambda c, t: (c // per_plane, jnp.minimum((t + 1) * (tt // HALO), T // HALO - 1),
                                             c % per_plane)),
                  cur,
                  pl.BlockSpec((HALO, CONV_TW), lambda c, t: (jnp.maximum(t * (tt // HALO) - 1, 0), c)),
                  pl.BlockSpec((CONV_K, CONV_TW), lambda c, t: (0, c))],
        out_specs=[cur, pl.BlockSpec((CONV_K, CONV_TW), lambda c, t: (0, c))],
        out_shape=[jax.ShapeDtypeStruct((T, W), BF16), jax.ShapeDtypeStruct((CONV_K, W), F32)],
        sem=("parallel", "arbitrary"), name="conv_bwd", carried=carried)
    return dx, dw, moved


def _dot(a, b, kind, prec):
    nb = a.ndim - 2
    batch = tuple(range(nb))
    ca = nb if kind == "tn" else nb + 1
    cb = nb + 1 if kind == "nt" else nb
    dims = (((ca,), (cb,)), (batch, batch))
    if prec == "bf16":
        return lax.dot_general(a.astype(BF16), b.astype(BF16), dims, preferred_element_type=F32)
    return lax.dot_general(a, b, dims, precision=HI if prec == "f32" else lax.Precision.HIGH,
                           preferred_element_type=F32)


@functools.lru_cache(maxsize=None)
def _mm(kind, prec):
    @jax.custom_vjp
    def f(a, b):
        return _dot(a, b, kind, prec)

    def fwd(a, b):
        return f(a, b), (a, b)

    def bwd(res, ct):
        a, b = res
        if kind == "nn":
            return _dot(ct, b, "nt", prec), _dot(a, ct, "tn", prec)
        if kind == "nt":
            return _dot(ct, b, "nn", prec), _dot(ct, a, "tn", prec)
        return _dot(b, ct, "nt", prec), _dot(a, ct, "nn", prec)

    f.defvjp(fwd, bwd)
    return f


def _col_from_row(row, eye):
    return jnp.sum(jnp.where(eye, row, 0.0), axis=2, keepdims=True)


def _silu(x):
    return x / (1.0 + jnp.exp(-x))


def _gdn_chunk(yq, yk, yv, z, a_row, b_row, a_log, dt_bias, norm_g, state):
    h = yq.shape[0]
    ii = lax.broadcasted_iota(jnp.int32, (1, CHUNK, CHUNK), 1)
    jj = lax.broadcasted_iota(jnp.int32, (1, CHUNK, CHUNK), 2)
    eye = ii == jj
    qr, kr, v = _silu(yq), _silu(yk), _silu(yv)
    q = qr * lax.rsqrt(jnp.sum(qr * qr, axis=-1, keepdims=True) + NORM_EPS) * (HEAD_DIM ** -0.5)
    k = kr * lax.rsqrt(jnp.sum(kr * kr, axis=-1, keepdims=True) + NORM_EPS)
    beta_row = _sigmoid(b_row)
    xa = a_row + dt_bias
    g_row = -jnp.exp(a_log) * (jnp.maximum(xa, 0.0) + jnp.log(1.0 + jnp.exp(-jnp.abs(xa))))
    upper = jnp.broadcast_to(jnp.where(ii <= jj, 1.0, 0.0).astype(F32), (h, CHUNK, CHUNK))
    decay_row = _mm("nn", "f32")(g_row, upper)
    decay_col = _col_from_row(decay_row, eye)
    beta_col = _col_from_row(beta_row, eye)
    decay_last = jnp.sum(g_row, axis=2, keepdims=True)
    gamma = jnp.exp(jnp.where(ii >= jj, decay_col - decay_row, -jnp.inf))
    k_beta = k * beta_col
    a_low = jnp.where(ii > jj, _mm("nt", GDN_PREC_SOLVE)(k_beta, k) * gamma, 0.0)
    t_inv = jnp.where(eye, 1.0, 0.0).astype(F32) - a_low
    p = a_low
    for _ in range(5):
        p = _mm("nn", GDN_PREC_SOLVE)(p, p)
        t_inv = t_inv + _mm("nn", GDN_PREC_SOLVE)(t_inv, p)
    e_dec = jnp.exp(decay_col)
    u = _mm("nn", GDN_PREC_SOLVE)(t_inv, v * beta_col)
    w = _mm("nn", GDN_PREC_SOLVE)(t_inv, k_beta * e_dec)
    qk = _mm("nt", GDN_PREC)(q, k) * gamma
    v_new = u - _mm("nn", GDN_PREC)(w, state)
    o = _mm("nn", GDN_PREC)(q * e_dec, state) + _mm("nn", GDN_PREC)(qk, v_new)
    s_new = state * jnp.exp(decay_last) + _mm("tn", GDN_PREC)(k * jnp.exp(decay_last - decay_col), v_new)
    o_n = o * lax.rsqrt(jnp.mean(o * o, axis=-1, keepdims=True) + NORM_EPS) * norm_g
    return o_n * _silu(z), s_new


def _heads(ref, hb):
    return jnp.stack([ref[:, HEAD_DIM * i:HEAD_DIM * (i + 1)] for i in range(hb)], axis=0)


def _gdn_fwd(y, proj, a_rows, b_rows, a_log, dt_bias, norm_g, carried=None):
    T = y.shape[0]
    H, HB = GDN_HEADS, GDN_HB
    NC, HG, BW = T // CHUNK, GDN_HEADS // GDN_HB, GDN_HB * HEAD_DIM

    def body(q_ref, k_ref, v_ref, z_ref, a_ref, b_ref, alog_ref, dt_ref, ng_ref, o_ref, sst_ref, s_scr):
        @pl.when(pl.program_id(1) == 0)
        def _():
            s_scr[...] = jnp.zeros_like(s_scr)

        state = s_scr[...]
        sst_ref[...] = state
        o_g, s_new = _gdn_chunk(_heads(q_ref, HB), _heads(k_ref, HB), _heads(v_ref, HB), _heads(z_ref, HB),
                                a_ref[...], b_ref[...], alog_ref[...], dt_ref[...], ng_ref[...], state)
        s_scr[...] = s_new
        for i in range(HB):
            o_ref[:, HEAD_DIM * i:HEAD_DIM * (i + 1)] = o_g[i].astype(BF16)

    col = lambda off: pl.BlockSpec((CHUNK, BW), lambda hg, n, off=off: (n, off + hg))
    row = pl.BlockSpec((HB, None, 1, CHUNK), lambda hg, n: (hg, n, 0, 0))
    sc = pl.BlockSpec((HB, 1, 1), lambda hg, n: (hg, 0, 0))
    (o, states), moved = _call(
        body, (y, y, y, proj, a_rows, b_rows, a_log, dt_bias, norm_g), grid=(HG, NC),
        in_specs=[col(0), col(HG), col(2 * HG), col(Z0 // BW), row, row, sc, sc,
                  pl.BlockSpec((1, 1, HEAD_DIM), lambda hg, n: (0, 0, 0))],
        out_specs=[col(0), pl.BlockSpec((HB, None, HEAD_DIM, HEAD_DIM), lambda hg, n: (hg, n, 0, 0))],
        out_shape=[jax.ShapeDtypeStruct((T, GDN_W), BF16), jax.ShapeDtypeStruct((H, NC, HEAD_DIM, HEAD_DIM), F32)],
        scratch_shapes=[pltpu.VMEM((HB, HEAD_DIM, HEAD_DIM), F32)],
        sem=("parallel", "arbitrary"), name="gdn_fwd", carried=carried)
    return o, states, moved


def _gdn_bwd(y, proj, a_rows, b_rows, a_log, dt_bias, norm_g, states, do, carried=None):
    T = y.shape[0]
    H, HB = GDN_HEADS, GDN_HB
    NC, HG, BW = T // CHUNK, GDN_HEADS // GDN_HB, GDN_HB * HEAD_DIM

    def body(q_ref, k_ref, v_ref, z_ref, a_ref, b_ref, alog_ref, dt_ref, ng_ref, sst_ref, do_ref,
             dy_ref, dz_ref, da_ref, db_ref, dalog_ref, ddt_ref, dng_ref, ds_scr):
        hg, n = pl.program_id(0), pl.program_id(1)

        @pl.when(n == 0)
        def _():
            ds_scr[...] = jnp.zeros_like(ds_scr)
            dalog_ref[...] = jnp.zeros_like(dalog_ref)
            ddt_ref[...] = jnp.zeros_like(ddt_ref)

        @pl.when((n == 0) & (hg == 0))
        def _():
            dng_ref[...] = jnp.zeros_like(dng_ref)

        args = (_heads(q_ref, HB), _heads(k_ref, HB), _heads(v_ref, HB), _heads(z_ref, HB), a_ref[...], b_ref[...],
                alog_ref[...], dt_ref[...], ng_ref[...], sst_ref[...])
        _, vjp = jax.vjp(_gdn_chunk, *args)
        dq, dk, dv, dz, da, db, dalog, ddt, dng, d_state = vjp((_heads(do_ref, HB).astype(F32), ds_scr[...]))
        ds_scr[...] = d_state
        for i in range(HB):
            sl = slice(HEAD_DIM * i, HEAD_DIM * (i + 1))
            dy_ref[0, :, sl] = dq[i]
            dy_ref[1, :, sl] = dk[i]
            dy_ref[2, :, sl] = dv[i]
            dz_ref[:, sl] = dz[i].astype(BF16)
        da_ref[...] = da
        db_ref[...] = db
        dalog_ref[...] += dalog
        ddt_ref[...] += ddt
        dng_ref[...] += dng

    rev = lambda n: NC - 1 - n
    col = lambda off: pl.BlockSpec((CHUNK, BW), lambda hg, n, off=off: (rev(n), off + hg))
    row = pl.BlockSpec((HB, None, 1, CHUNK), lambda hg, n: (hg, rev(n), 0, 0))
    sc = pl.BlockSpec((HB, 1, 1), lambda hg, n: (hg, 0, 0))
    ng = pl.BlockSpec((1, 1, HEAD_DIM), lambda hg, n: (0, 0, 0))
    sst = pl.BlockSpec((HB, None, HEAD_DIM, HEAD_DIM), lambda hg, n: (hg, rev(n), 0, 0))
    rw = jax.ShapeDtypeStruct((H, NC, 1, CHUNK), F32)
    s1 = jax.ShapeDtypeStruct((H, 1, 1), F32)
    (dy, dz, da, db, dalog, ddt, dng), moved = _call(
        body, (y, y, y, proj, a_rows, b_rows, a_log, dt_bias, norm_g, states, do), grid=(HG, NC),
        in_specs=[col(0), col(HG), col(2 * HG), col(Z0 // BW), row, row, sc, sc, ng, sst, col(0)],
        out_specs=[pl.BlockSpec((3, CHUNK, BW), lambda hg, n: (0, rev(n), hg)), col(0), row, row, sc, sc, ng],
        out_shape=[jax.ShapeDtypeStruct((3, T, GDN_W), F32), jax.ShapeDtypeStruct((T, GDN_W), BF16), rw, rw, s1, s1,
                   jax.ShapeDtypeStruct((1, 1, HEAD_DIM), F32)],
        scratch_shapes=[pltpu.VMEM((HB, HEAD_DIM, HEAD_DIM), F32)],
        sem=("arbitrary", "arbitrary"), name="gdn_bwd", carried=carried)
    return dy, dz, da, db, dalog, ddt, dng, moved


def _swa_block(q, k_prev, k_cur, v_prev, v_cur, sink, slope, has_prev):
    kb = jnp.concatenate([k_prev, k_cur], axis=0)
    vb = jnp.concatenate([v_prev, v_cur], axis=0)
    q2 = q.reshape(SWA_GROUP * WINDOW, SWA_DIM)
    s = _mm("nt", SWA_PREC)(q2, kb)
    s = s.reshape(SWA_GROUP, WINDOW, 2 * WINDOW) * (SWA_DIM ** -0.5)
    qi = lax.broadcasted_iota(jnp.int32, (1, WINDOW, 2 * WINDOW), 1)
    sj = lax.broadcasted_iota(jnp.int32, (1, WINDOW, 2 * WINDOW), 2)
    dist = qi + WINDOW - sj
    valid = (dist >= 0) & (dist < WINDOW) & (has_prev | (sj >= WINDOW))
    s = jnp.where(valid, s - slope * dist.astype(F32), -jnp.inf)
    m = lax.stop_gradient(jnp.maximum(jnp.max(s, axis=-1, keepdims=True), sink))
    p = jnp.exp(s - m)
    probs = p / (jnp.sum(p, axis=-1, keepdims=True) + jnp.exp(sink - m))
    o = _mm("nn", SWA_PREC)(probs.reshape(SWA_GROUP * WINDOW, 2 * WINDOW), vb)
    return o.reshape(SWA_GROUP, WINDOW, SWA_DIM)


def _alibi_slopes():
    return (2.0 ** (-8.0 * jnp.arange(1, SWA_HEADS + 1, dtype=F32) / SWA_HEADS)).reshape(SWA_HEADS, 1, 1)


def _swa_fwd(q, k, v, sinks, carried=None):
    T = q.shape[1]
    NB = T // WINDOW

    KS, G = SWA_KV_STEP, SWA_GROUP

    def body(q_ref, kp_ref, kc_ref, vp_ref, vc_ref, sink_ref, slope_ref, o_ref):
        for i in range(KS):
            hq = slice(G * i, G * (i + 1))
            o = _swa_block(q_ref[hq], kp_ref[i], kc_ref[i], vp_ref[i], vc_ref[i], sink_ref[hq], slope_ref[hq],
                           pl.program_id(1) > 0)
            o_ref[hq] = o.astype(BF16)

    qs = pl.BlockSpec((KS * G, WINDOW, SWA_DIM), lambda h, n: (h, n, 0))
    cur = pl.BlockSpec((KS, WINDOW, SWA_DIM), lambda h, n: (h, n, 0))
    prev = pl.BlockSpec((KS, WINDOW, SWA_DIM), lambda h, n: (h, jnp.maximum(n - 1, 0), 0))
    hs = pl.BlockSpec((KS * G, 1, 1), lambda h, n: (h, 0, 0))
    (o,), moved = _call(
        body, (q, k, k, v, v, sinks, _alibi_slopes()), grid=(SWA_KV // KS, NB),
        in_specs=[qs, prev, cur, prev, cur, hs, hs], out_specs=[qs],
        out_shape=[jax.ShapeDtypeStruct((SWA_HEADS, T, SWA_DIM), BF16)],
        sem=("parallel", "parallel"), name="swa_fwd", carried=carried)
    return o, moved


def _swa_bwd(q, k, v, sinks, do, carried=None):
    T = q.shape[1]
    NB = T // WINDOW

    def body(q_ref, kp_ref, kc_ref, vp_ref, vc_ref, sink_ref, slope_ref, do_ref,
             dq_ref, dk_ref, dv_ref, dsink_ref, dk_scr, dv_scr):
        n = pl.program_id(1)

        @pl.when(n == 0)
        def _():
            dk_scr[...] = jnp.zeros_like(dk_scr)
            dv_scr[...] = jnp.zeros_like(dv_scr)
            dsink_ref[...] = jnp.zeros_like(dsink_ref)

        has_prev = n < NB - 1
        for i in range(KS):
            hq = slice(G * i, G * (i + 1))
            fn = functools.partial(_swa_block, slope=slope_ref[hq], has_prev=has_prev)
            _, vjp = jax.vjp(fn, q_ref[hq], kp_ref[i], kc_ref[i], vp_ref[i], vc_ref[i], sink_ref[hq])
            dq, dkp, dkc, dvp, dvc, dsink = vjp(do_ref[hq].astype(F32))
            dq_ref[hq] = dq.astype(BF16)
            dk_ref[i] = (dkc + dk_scr[i]).astype(BF16)
            dv_ref[i] = (dvc + dv_scr[i]).astype(BF16)
            dk_scr[i] = dkp
            dv_scr[i] = dvp
            dsink_ref[hq] += dsink

    KS, G = SWA_KV_STEP, SWA_GROUP
    rev = lambda n: NB - 1 - n
    qs = pl.BlockSpec((KS * G, WINDOW, SWA_DIM), lambda h, n: (h, rev(n), 0))
    cur = pl.BlockSpec((KS, WINDOW, SWA_DIM), lambda h, n: (h, rev(n), 0))
    prev = pl.BlockSpec((KS, WINDOW, SWA_DIM), lambda h, n: (h, jnp.maximum(rev(n) - 1, 0), 0))
    hs = pl.BlockSpec((KS * G, 1, 1), lambda h, n: (h, 0, 0))
    kv = jax.ShapeDtypeStruct((SWA_KV, T, SWA_DIM), BF16)
    (dq, dk, dv, dsink), moved = _call(
        body, (q, k, k, v, v, sinks, _alibi_slopes(), do), grid=(SWA_KV // KS, NB),
        in_specs=[qs, prev, cur, prev, cur, hs, hs, qs], out_specs=[qs, cur, cur, hs],
        out_shape=[jax.ShapeDtypeStruct((SWA_HEADS, T, SWA_DIM), BF16), kv, kv,
                   jax.ShapeDtypeStruct((SWA_HEADS, 1, 1), F32)],
        scratch_shapes=[pltpu.VMEM((KS, WINDOW, SWA_DIM), F32), pltpu.VMEM((KS, WINDOW, SWA_DIM), F32)],
        sem=("parallel", "arbitrary"), name="swa_bwd", carried=carried)
    return dq, dk, dv, dsink, moved


def _to_heads(t, n_heads):
    return t.reshape(t.shape[0], n_heads, SWA_DIM).transpose(1, 0, 2)


def _from_heads(t):
    return t.transpose(1, 0, 2).reshape(t.shape[1], -1)


def _rows(t):
    return t.T.reshape(t.shape[1], t.shape[0] // CHUNK, 1, CHUNK)


def _unrows(t):
    return t.reshape(t.shape[0], -1).T


def _branch_inputs(proj):
    b_rows = _rows(proj[:, BA0:BA0 + GDN_HEADS])
    a_rows = _rows(proj[:, BA0 + GDN_HEADS:BA0 + 2 * GDN_HEADS])
    q_s = _to_heads(proj[:, QS0:QS0 + D_MODEL], SWA_HEADS)
    k_s = _to_heads(proj[:, KS0:KS0 + KV_W], SWA_KV)
    v_s = _to_heads(proj[:, VS0:VS0 + KV_W], SWA_KV)
    return a_rows, b_rows, q_s, k_s, v_s


def _relu2_epilogue(acc):
    r = jnp.maximum(acc, 0.0)
    return acc, r * r


_BRANCH_OUT = ("w_branch_gdn", "w_branch_swa", "w_out")
_FIRST_NEEDED = ("w_in", "conv_w")


def _gather_plan(l):
    nxt = l + 1
    if l == 0:
        plan = {"mm_proj": [(0, k) for k in _BRANCH_OUT + ("w_ff_up",)], "conv_fwd": [(0, "w_ff_down")],
                "gdn_fwd": [(nxt, k) for k in _FIRST_NEEDED], "swa_fwd": [(nxt, k) for k in _BRANCH_OUT],
                "mm_up": [(nxt, "w_ff_up")], "mm_down": [(nxt, "w_ff_down")]}
    elif nxt < DEPTH:
        plan = {"mm_proj": [(nxt, k) for k in _FIRST_NEEDED], "gdn_fwd": [(nxt, k) for k in _BRANCH_OUT],
                "swa_fwd": [(nxt, "w_ff_up")], "mm_up": [(nxt, "w_ff_down")]}
    else:
        plan = {}
    return plan


def _scatter_plan(l):
    if l == DEPTH - 1:
        return {"swa_bwd": ["w_ff_up"], "gdn_bwd": ["w_ff_down"] + list(_BRANCH_OUT)}
    return {"swa_bwd": ["w_ff_up"], "gdn_bwd": ["late", "w_ff_down"], "mm_proj_dw": list(_BRANCH_OUT)}


class _Weights:
    def __init__(self, params):
        self.params, self.full = params, {}

    def put(self, layer, name, gathered):
        self.full[layer, name] = _align_w_in(gathered) if name == "w_in" else _unblock(gathered, _SHARDED[name])

    def get(self, layer, name):
        return self.full[layer, name] if name in _SHARDED else self.params[name][layer]


def _layer_fwd(x, l, weights, shards):
    plan = _gather_plan(l)
    w = functools.partial(weights.get, l)

    def carry(host):
        items = plan.get(host)
        return _Exchange("gather", [shards[i][k] for i, k in items]) if items else None

    def keep(host, moved):
        for (i, k), gathered in zip(plan.get(host, ()), moved or ()):
            weights.put(i, k, gathered)

    h1 = _rms_fwd(x, w("norm1_g"), "rms1_fwd")
    proj, *moved = _matmul(h1, w("w_in"), "nn", [F32], name="mm_proj", carried=carry("mm_proj"))
    keep("mm_proj", moved[0] if moved else None)
    a_rows, b_rows, q_s, k_s, v_s = _branch_inputs(proj)
    y, moved = _conv_fwd(proj, w("conv_w"), carried=carry("conv_fwd"))
    keep("conv_fwd", moved)
    o_gdn, states, moved = _gdn_fwd(y, proj, a_rows, b_rows, w("a_log").reshape(-1, 1, 1),
                                    w("dt_bias").reshape(-1, 1, 1), w("gdn_norm_g").reshape(1, 1, -1),
                                    carried=carry("gdn_fwd"))
    keep("gdn_fwd", moved)
    o_swa, moved = _swa_fwd(q_s, k_s, v_s, w("attn_sinks").reshape(-1, 1, 1), carried=carry("swa_fwd"))
    keep("swa_fwd", moved)
    o_swa = _from_heads(o_swa)
    y_gdn, = _matmul(o_gdn, w("w_branch_gdn"), "nn", [F32], name="mm_bgdn")
    y_swa, = _matmul(o_swa, w("w_branch_swa"), "nn", [F32], name="mm_bswa")
    mix = _gate_mix_fwd(proj, y_gdn, y_swa)
    x2, = _matmul(mix, w("w_out"), "nn", [F32], epilogue=lambda acc, r: (r + acc,), extras=(x,), name="mm_out")
    h2 = _rms_fwd(x2, w("norm2_g"), "rms2_fwd")
    u, act, *moved = _matmul(h2, w("w_ff_up"), "nn", [F32, BF16], epilogue=_relu2_epilogue, name="mm_up",
                             carried=carry("mm_up"))
    keep("mm_up", moved[0] if moved else None)
    x3, *moved = _matmul(act, w("w_ff_down"), "nn", [F32], epilogue=lambda acc, r: (r + acc,), extras=(x2,),
                         name="mm_down", carried=carry("mm_down"))
    keep("mm_down", moved[0] if moved else None)
    saved = dict(x=x, h1=h1, proj=proj, y=y, states=states, o_gdn=o_gdn, o_swa=o_swa, y_gdn=y_gdn, y_swa=y_swa,
                 mix=mix, x2=x2, h2=h2, u=u, act=act)
    return x3, saved


def _layer_bwd(dx3, dx3b, l, weights, s, late, landed):
    T = dx3.shape[0]
    g = {}
    plan = _scatter_plan(l)
    w = functools.partial(weights.get, l)

    def carry(host):
        arrays = []
        for k in plan.get(host, ()):
            arrays += late if k == "late" else [_block(g[k], _SHARDED[k])]
        return _Exchange("scatter", arrays) if arrays else None

    def keep(host, moved):
        names = []
        for k in plan.get(host, ()):
            names += [(l + 1, n) for n in _FIRST_NEEDED] if k == "late" else [(l, k)]
        for (i, k), blocks in zip(names, moved or ()):
            landed[i][k] = blocks

    du, = _matmul(dx3b, w("w_ff_down"), "nt", [BF16], extras=(s["u"],),
                  epilogue=lambda acc, uu: (acc * 2.0 * jnp.maximum(uu, 0.0),), name="mm_down_dx")
    g["w_ff_down"], = _matmul(s["act"], dx3b, "tn", [BF16], name="mm_down_dw")
    dh2, = _matmul(du, w("w_ff_up"), "nt", [F32], name="mm_up_dx")
    g["w_ff_up"], = _matmul(s["h2"], du, "tn", [BF16], name="mm_up_dw")
    dx2, dx2b, g["norm2_g"] = _rms_bwd(dh2, s["x2"], w("norm2_g"), dx3, "rms2_bwd")
    dmix, = _matmul(dx2b, w("w_out"), "nt", [F32], name="mm_out_dx")
    g["w_out"], = _matmul(s["mix"], dx2b, "tn", [BF16], name="mm_out_dw")
    dy_gdn, dy_swa, dgg, dgs = _gate_mix_bwd(dmix, s["proj"], s["y_gdn"], s["y_swa"])
    do_gdn, = _matmul(dy_gdn, w("w_branch_gdn"), "nt", [BF16], name="mm_bgdn_dx")
    g["w_branch_gdn"], = _matmul(s["o_gdn"], dy_gdn, "tn", [BF16], name="mm_bgdn_dw")
    do_swa, = _matmul(dy_swa, w("w_branch_swa"), "nt", [BF16], name="mm_bswa_dx")
    g["w_branch_swa"], = _matmul(s["o_swa"], dy_swa, "tn", [BF16], name="mm_bswa_dw")
    a_rows, b_rows, q_s, k_s, v_s = _branch_inputs(s["proj"])
    dq_s, dk_s, dv_s, dsink, moved = _swa_bwd(q_s, k_s, v_s, w("attn_sinks").reshape(-1, 1, 1),
                                              _to_heads(do_swa, SWA_HEADS), carried=carry("swa_bwd"))
    keep("swa_bwd", moved)
    g["attn_sinks"] = dsink.reshape(-1)
    dy, dz, da_rows, db_rows, dalog, ddt, dng, moved = _gdn_bwd(
        s["y"], s["proj"], a_rows, b_rows, w("a_log").reshape(-1, 1, 1), w("dt_bias").reshape(-1, 1, 1),
        w("gdn_norm_g").reshape(1, 1, -1), s["states"], do_gdn, carried=carry("gdn_bwd"))
    keep("gdn_bwd", moved)
    g["a_log"], g["dt_bias"], g["gdn_norm_g"] = dalog.reshape(-1), ddt.reshape(-1), dng.reshape(-1)
    dqkv, g["conv_w"], _ = _conv_bwd(dy, s["proj"], w("conv_w"))
    dba = jnp.concatenate([_unrows(db_rows), _unrows(da_rows)], axis=1).astype(BF16)
    dproj = jnp.concatenate(
        [dqkv, dz, dgg, dgs, _from_heads(dq_s), _from_heads(dk_s), _from_heads(dv_s), dba,
         jnp.zeros((T, N_PROJ - BA0 - 2 * GDN_HEADS), BF16)], axis=1)
    dh1, *moved = _matmul(dproj, w("w_in"), "nt", [F32], name="mm_proj_dx", carried=carry("mm_proj_dx"))
    keep("mm_proj_dx", moved[0] if moved else None)
    g["w_in"], *moved = _matmul(s["h1"], dproj, "tn", [BF16], name="mm_proj_dw", carried=carry("mm_proj_dw"))
    keep("mm_proj_dw", moved[0] if moved else None)
    dx, dxb, g["norm1_g"] = _rms_bwd(dh1, s["x"], w("norm1_g"), dx2, "rms1_bwd")
    g["norm1_g"], g["norm2_g"] = g["norm1_g"].reshape(-1), g["norm2_g"].reshape(-1)
    own_late = [_block_w_in(g["w_in"]), _block(g["conv_w"], _SHARDED["conv_w"]).astype(BF16)]
    return dx, dxb, g, own_late


def _all_reduce_small(part):
    R = part.shape[0]

    def body(x_ref, sum_ref, gath, send_sems, recv_sems):
        me = _my_index()
        gath[me] = x_ref[...]
        sends = []
        for mask in range(1, N_DEV):
            dev, _ = _peer(mask)
            sends.append(pltpu.make_async_remote_copy(
                src_ref=x_ref, dst_ref=gath.at[me], send_sem=send_sems.at[mask - 1], recv_sem=recv_sems.at[mask - 1],
                device_id=dev, device_id_type=MESH))
        for cp in sends:
            cp.start()
        for mask in range(1, N_DEV):
            dev, idx = _peer(mask)
            pltpu.make_async_remote_copy(
                src_ref=x_ref, dst_ref=gath.at[idx], send_sem=send_sems.at[mask - 1], recv_sem=recv_sems.at[mask - 1],
                device_id=dev, device_id_type=MESH).wait_recv()
        for cp in sends:
            cp.wait_send()
        acc = gath[0]
        for i in range(1, N_DEV):
            acc = acc + gath[i]
        sum_ref[...] = acc

    vm = pl.BlockSpec(memory_space=pltpu.VMEM)
    return pl.pallas_call(
        body, in_specs=[vm], out_specs=vm, out_shape=jax.ShapeDtypeStruct((R, 128), F32),
        scratch_shapes=[pltpu.VMEM((N_DEV, R, 128), F32), pltpu.SemaphoreType.DMA((7,)), pltpu.SemaphoreType.DMA((7,))],
        name="small_all_reduce",
    )(part)


def _adam_math(w, g, m, v):
    m = ADAM_B1 * m + (1.0 - ADAM_B1) * g
    v = ADAM_B2 * v + (1.0 - ADAM_B2) * (g * g)
    m_hat = m / (1.0 - ADAM_B1 ** ADAM_STEP)
    v_hat = v / (1.0 - ADAM_B2 ** ADAM_STEP)
    delta = -ADAM_LR * (m_hat / (jnp.sqrt(v_hat) + ADAM_EPS) + ADAM_WD * w)
    return delta, m, v


def _adamw_sum(parts, w, m, v, name):
    n_layers = len(parts)
    R, Cc = parts[0].shape[1:]
    tr = R
    while tr * Cc * 4 > (1 << 20) and tr % 32 == 0:
        tr //= 2
    nblk = R // tr

    def body(*refs):
        p_refs = refs[:n_layers]
        w_ref, m_ref, v_ref, g_ref, d_ref, nm_ref, nv_ref = refs[n_layers:]
        for j in range(n_layers):
            @pl.when(pl.program_id(0) == j)
            def _(p_ref=p_refs[j]):
                g = p_ref[0].astype(F32)
                for i in range(1, N_DEV):
                    g = g + p_ref[i].astype(F32)
                d, nm, nv = _adam_math(w_ref[...], g, m_ref[...], v_ref[...])
                g_ref[...], d_ref[...], nm_ref[...], nv_ref[...] = g, d, nm, nv

    def part_spec(j):
        return pl.BlockSpec((N_DEV, tr, Cc), lambda l, i: (0, jnp.where(l == j, i, jnp.where(l < j, 0, nblk - 1)), 0))

    blk = pl.BlockSpec((tr, Cc), lambda l, i: (l * nblk + i, 0))
    out = jax.ShapeDtypeStruct((n_layers * R, Cc), F32)
    return pl.pallas_call(
        body, grid=(n_layers, nblk), in_specs=[part_spec(j) for j in range(n_layers)] + [blk, blk, blk],
        out_specs=[blk] * 4, out_shape=[out] * 4, compiler_params=_cp("arbitrary", "arbitrary"), name=name,
    )(*parts, w, m, v)


def _adamw_small(g, w, m, v):
    def body(g_ref, w_ref, m_ref, v_ref, d_ref, nm_ref, nv_ref):
        d_ref[...], nm_ref[...], nv_ref[...] = _adam_math(w_ref[...], g_ref[...], m_ref[...], v_ref[...])

    out = jax.ShapeDtypeStruct(g.shape, F32)
    return pl.pallas_call(body, out_shape=[out] * 3, name="adamw_small")(g, w, m, v)


def _w_in_pieces():
    per = sum(width for _, width, _ in _SEGS) // N_DEV
    pieces = []
    for src, width, dst in _SEGS:
        c = src
        while c < src + width:
            d = c // per
            n = min(src + width, (d + 1) * per) - c
            pieces.append((d, c - d * per, dst + c - src, n))
            c += n
    return pieces


def _align_w_in(gathered):
    pieces = sorted(_w_in_pieces(), key=lambda p: p[2])
    end = pieces[-1][2] + pieces[-1][3]
    parts = [gathered[d, :, a:a + n] for d, a, _, n in pieces]
    return jnp.concatenate(parts + [jnp.zeros((gathered.shape[1], N_PROJ - end), gathered.dtype)], axis=1)


def _block_w_in(g):
    pieces = sorted(_w_in_pieces(), key=lambda p: (p[0], p[1]))
    return jnp.stack([jnp.concatenate([g[:, dst:dst + n] for d, _, dst, n in pieces if d == dev], axis=1)
                      for dev in range(N_DEV)])


_SHARDED = {"w_in": 1, "conv_w": 1, "w_branch_gdn": 0, "w_branch_swa": 0, "w_out": 0, "w_ff_up": 1, "w_ff_down": 0}
_SMALL = ("norm1_g", "a_log", "dt_bias", "gdn_norm_g", "attn_sinks", "norm2_g", "final_norm_g")


def _unblock(gathered, axis):
    if axis == 0:
        return gathered.reshape(-1, gathered.shape[2])
    return gathered.transpose(1, 0, 2).reshape(gathered.shape[1], -1)


def _block(full, axis):
    A, B = full.shape
    if axis == 0:
        return full.reshape(N_DEV, A // N_DEV, B)
    return full.reshape(A, N_DEV, B // N_DEV).transpose(1, 0, 2)


def _pack_small(d, loss_row=None):
    rows = [d[k].astype(F32).reshape(-1, 128) if d[k].size % 128 == 0 else
            jnp.pad(d[k].astype(F32), ((0, 0), (0, 128 - d[k].shape[-1]))) for k in _SMALL]
    rows.append(jnp.zeros((1, 128), F32) if loss_row is None else loss_row)
    packed = jnp.concatenate(rows, axis=0)
    return jnp.pad(packed, ((0, -packed.shape[0] % 8), (0, 0)))


def _unpack_small(packed, like):
    out, r = {}, 0
    for k in _SMALL:
        shp = like[k].shape
        if like[k].size % 128 == 0:
            n = like[k].size // 128
            out[k] = packed[r:r + n].reshape(shp)
        else:
            n = shp[0]
            out[k] = packed[r:r + n, :shp[-1]]
        r += n
    return out, packed[r, 0]


def kernel(x, norm1_g, w_in, conv_w, a_log, dt_bias, gdn_norm_g, attn_sinks, w_branch_gdn, w_branch_swa, w_out, norm2_g, w_ff_up, w_ff_down, final_norm_g, loss_target, m_norm1_g, m_w_in, m_conv_w, m_a_log, m_dt_bias, m_gdn_norm_g, m_attn_sinks, m_w_branch_gdn, m_w_branch_swa, m_w_out, m_norm2_g, m_w_ff_up, m_w_ff_down, m_final_norm_g, v_norm1_g, v_w_in, v_conv_w, v_a_log, v_dt_bias, v_gdn_norm_g, v_attn_sinks, v_w_branch_gdn, v_w_branch_swa, v_w_out, v_norm2_g, v_w_ff_up, v_w_ff_down, v_final_norm_g):
    names = ("norm1_g", "w_in", "conv_w", "a_log", "dt_bias", "gdn_norm_g", "attn_sinks", "w_branch_gdn",
             "w_branch_swa", "w_out", "norm2_g", "w_ff_up", "w_ff_down", "final_norm_g")
    w = dict(zip(names, (norm1_g, w_in, conv_w, a_log, dt_bias, gdn_norm_g, attn_sinks, w_branch_gdn, w_branch_swa,
                         w_out, norm2_g, w_ff_up, w_ff_down, final_norm_g)))
    m = dict(zip(names, (m_norm1_g, m_w_in, m_conv_w, m_a_log, m_dt_bias, m_gdn_norm_g, m_attn_sinks, m_w_branch_gdn,
                         m_w_branch_swa, m_w_out, m_norm2_g, m_w_ff_up, m_w_ff_down, m_final_norm_g)))
    v = dict(zip(names, (v_norm1_g, v_w_in, v_conv_w, v_a_log, v_dt_bias, v_gdn_norm_g, v_attn_sinks, v_w_branch_gdn,
                         v_w_branch_swa, v_w_out, v_norm2_g, v_w_ff_up, v_w_ff_down, v_final_norm_g)))
    sharded = list(_SHARDED)

    shards = [{k: w[k][l] if k == "conv_w" else w[k][l].astype(BF16) for k in sharded} for l in range(DEPTH)]

    weights = _Weights(w)
    for k, gathered in zip(_FIRST_NEEDED, _exchange("gather", [shards[0][k] for k in _FIRST_NEEDED], "weight_all_gather")):
        weights.put(0, k, gathered)
    xc, saved = x[0], []
    for l in range(DEPTH):
        xc, s = _layer_fwd(xc, l, weights, shards)
        saved.append(s)
    loss_row, dx, dxb, dgf = _loss_head(xc, final_norm_g, loss_target[0])

    grads, landed, late = [None] * DEPTH, [{} for _ in range(DEPTH)], None
    for l in reversed(range(DEPTH)):
        dx, dxb, grads[l], late = _layer_bwd(dx, dxb, l, weights, saved[l], late, landed)
    landed[0].update(zip(_FIRST_NEEDED, _exchange("scatter", late, "grad_scatter")))
    grad_x = dx
    out_g, out_d, out_m, out_v = {}, {}, {}, {}
    for k in sharded:
        shp = w[k].shape
        flat = lambda t: t.reshape(-1, shp[-1])
        parts = [landed[l][k] for l in range(DEPTH)]
        if parts[0].shape[1] % 8:
            parts = [jnp.concatenate(parts, axis=1)]
        res = _adamw_sum(parts, flat(w[k]), flat(m[k]), flat(v[k]), "adamw_" + k)
        out_g[k], out_d[k], out_m[k], out_v[k] = [t.reshape(shp) for t in res]

    small_g = {k: jnp.stack([gl[k] for gl in grads]) for k in _SMALL if k != "final_norm_g"}
    small_g["final_norm_g"] = dgf.reshape(-1)
    total = _all_reduce_small(_pack_small(small_g, loss_row))
    sd, sm, sv = _adamw_small(total, _pack_small(w), _pack_small(m), _pack_small(v))
    g_small, loss = _unpack_small(total, w)
    d_small, _ = _unpack_small(sd, w)
    m_small, _ = _unpack_small(sm, w)
    v_small, _ = _unpack_small(sv, w)
    out_g.update(g_small), out_d.update(d_small), out_m.update(m_small), out_v.update(v_small)

    return (loss, grad_x[None], *[out_g[k] for k in names], *[out_d[k] for k in names],
            *[out_m[k] for k in names], *[out_v[k] for k in names])
```

```python
import functools

import jax
import jax.numpy as jnp
from jax import lax
from jax.experimental import pallas as pl
from jax.experimental.pallas import tpu as pltpu

F32 = jnp.float32
BF16 = jnp.bfloat16
HI = lax.Precision.HIGHEST
MESH = pl.DeviceIdType.MESH

N_DEV = 8
D_MODEL = 2048
DEPTH = 4
GDN_HEADS = 16
HEAD_DIM = 128
CHUNK = 64
CONV_K = 4
SWA_HEADS = 32
SWA_KV = 4
SWA_GROUP = SWA_HEADS // SWA_KV
SWA_DIM = 64
WINDOW = 128
NORM_EPS = 1e-6
GDN_W = GDN_HEADS * HEAD_DIM
KV_W = SWA_KV * SWA_DIM

QKV0, Z0, GG0, GS0, QS0, KS0, VS0, BA0 = 0, 6144, 8192, 10240, 12288, 14336, 14592, 14848
N_PROJ = 15360
_SEGS = ((0, 6144, QKV0), (6144, 2048, Z0), (8192, 32, BA0), (8224, 2048, QS0), (10272, 256, KS0),
         (10528, 256, VS0), (10784, 4096, GG0))

ADAM_LR, ADAM_B1, ADAM_B2, ADAM_EPS, ADAM_WD, ADAM_STEP = 0.001, 0.9, 0.999, 1e-08, 0.01, 10

VMEM_LIMIT = 56 * 1024 * 1024
GDN_HB = 16
GDN_PREC_SOLVE = "x3"
GDN_PREC = "x3"
SWA_PREC = "bf16"
SWA_KV_STEP = 4


def _cp(*sem):
    return pltpu.CompilerParams(dimension_semantics=sem, vmem_limit_bytes=VMEM_LIMIT)


def _tile(n, want):
    t = min(n, want)
    while n % t:
        t -= 128
    return t


def _my_index():
    return 4 * lax.axis_index("x") + 2 * lax.axis_index("y") + lax.axis_index("c")


def _peer(mask):
    x, y, c = lax.axis_index("x"), lax.axis_index("y"), lax.axis_index("c")
    px = 1 - x if mask & 4 else x
    py = 1 - y if mask & 2 else y
    pc = 1 - c if mask & 1 else c
    return (px, py, pc), 4 * px + 2 * py + pc


class _Exchange:
    def __init__(self, kind, arrays):
        self.kind, self.arrays, self.n = kind, list(arrays), len(arrays)
        self.out_shape = [jax.ShapeDtypeStruct(((N_DEV,) + a.shape) if kind == "gather" else a.shape, a.dtype)
                          for a in self.arrays]
        self.scratch = [pltpu.SemaphoreType.DMA((self.n, 7)), pltpu.SemaphoreType.DMA((self.n, 7)),
                        pltpu.SemaphoreType.DMA((self.n,))]

    def _local(self, ins, outs, sems):
        me = _my_index()
        src = (lambda a: ins[a]) if self.kind == "gather" else (lambda a: ins[a].at[me])
        return [pltpu.make_async_copy(src(a), outs[a].at[me], sems[2].at[a]) for a in range(self.n)]

    def _copy(self, outs, sems, a, k, src, block, to):
        return pltpu.make_async_remote_copy(src_ref=src, dst_ref=outs[a].at[block], send_sem=sems[0].at[a, k],
                                            recv_sem=sems[1].at[a, k], device_id=to, device_id_type=MESH)

    def _sends(self, ins, outs, sems):
        me = _my_index()
        cps = []
        for a in range(self.n):
            if self.kind == "gather":
                for k, mask in enumerate((1, 4, 2, 6)):
                    cps.append(self._copy(outs, sems, a, k, ins[a], me, _peer(mask)[0]))
            else:
                for mask in range(1, N_DEV):
                    dev, idx = _peer(mask)
                    cps.append(self._copy(outs, sems, a, mask - 1, ins[a].at[idx], me, dev))
        return cps

    def _relays(self, outs, sems):
        sibling = _peer(1)[0]
        return [self._copy(outs, sems, a, 4 + j, outs[a].at[_peer(mask)[1]], _peer(mask)[1], sibling)
                for a in range(self.n) for j, mask in enumerate((4, 2, 6))]

    def start(self, ins, outs, sems):
        for cp in self._local(ins, outs, sems) + self._sends(ins, outs, sems):
            cp.start()

    def finish(self, ins, outs, sems):
        me = _my_index()
        if self.kind == "gather":
            relays = self._relays(outs, sems)
            for a in range(self.n):
                for j, mask in enumerate((4, 2, 6)):
                    self._copy(outs, sems, a, 1 + j, ins[a], _peer(mask)[1], _peer(mask)[0]).wait_recv()
                    relays[3 * a + j].start()
            for a in range(self.n):
                self._copy(outs, sems, a, 0, ins[a], _peer(1)[1], _peer(1)[0]).wait_recv()
                for j, mask in enumerate((4, 2, 6)):
                    self._copy(outs, sems, a, 4 + j, ins[a], _peer(mask)[1] ^ 1, _peer(1)[0]).wait_recv()
            for cp in relays:
                cp.wait_send()
        else:
            for a in range(self.n):
                for mask in range(1, N_DEV):
                    dev, idx = _peer(mask)
                    self._copy(outs, sems, a, mask - 1, ins[a].at[idx], idx, dev).wait_recv()
        for cp in self._sends(ins, outs, sems):
            cp.wait_send()
        for cp in self._local(ins, outs, sems):
            cp.wait()


def _exchange(kind, arrays, name):
    ex = _Exchange(kind, arrays)

    def body(*refs):
        ins, outs, sems = refs[:ex.n], refs[ex.n:2 * ex.n], refs[2 * ex.n:]
        ex.start(ins, outs, sems)
        ex.finish(ins, outs, sems)

    any_spec = pl.BlockSpec(memory_space=pl.ANY)
    return pl.pallas_call(body, in_specs=[any_spec] * ex.n, out_specs=[any_spec] * ex.n, out_shape=ex.out_shape,
                          scratch_shapes=ex.scratch, name=name)(*ex.arrays)


def _call(body, args, *, grid, in_specs, out_specs, out_shape, scratch_shapes=(), sem, name, carried=None,
          aliases=None):
    aliases = aliases or {}
    if carried is None:
        outs = pl.pallas_call(body, grid=grid, in_specs=list(in_specs), out_specs=list(out_specs),
                              out_shape=list(out_shape), scratch_shapes=list(scratch_shapes),
                              input_output_aliases=aliases, compiler_params=_cp(*sem), name=name)(*args)
        return outs, None
    n_in, n_out, n_scr, n_c = len(args), len(out_shape), len(scratch_shapes), carried.n

    def wrapped(*refs):
        ins, c_in = refs[:n_in], refs[n_in:n_in + n_c]
        o0 = n_in + n_c
        outs, c_out = refs[o0:o0 + n_out], refs[o0 + n_out:o0 + n_out + n_c]
        s0 = o0 + n_out + n_c
        scr, c_sems = refs[s0:s0 + n_scr], refs[s0 + n_scr:]
        ids = [pl.program_id(i) for i in range(len(grid))]
        first = functools.reduce(jnp.logical_and, [i == 0 for i in ids])
        last = functools.reduce(jnp.logical_and, [i == g - 1 for i, g in zip(ids, grid)])

        @pl.when(first)
        def _():
            carried.start(c_in, c_out, c_sems)

        body(*ins, *outs, *scr)

        @pl.when(last)
        def _():
            carried.finish(c_in, c_out, c_sems)

    any_spec = pl.BlockSpec(memory_space=pl.ANY)
    res = pl.pallas_call(
        wrapped, grid=grid, in_specs=list(in_specs) + [any_spec] * n_c, out_specs=list(out_specs) + [any_spec] * n_c,
        out_shape=list(out_shape) + carried.out_shape, scratch_shapes=list(scratch_shapes) + carried.scratch,
        input_output_aliases=aliases, compiler_params=_cp(*["arbitrary"] * len(grid)), name=name)(*args, *carried.arrays)
    return res[:n_out], res[n_out:]


def _matmul(a, b, mode, out_shapes, epilogue=None, extras=(), name="mm", carried=None):
    if mode == "tn":
        K, M = a.shape
    else:
        M, K = a.shape
    N = b.shape[0] if mode == "nt" else b.shape[1]
    tm, tn, tk = _tile(M, 1024), _tile(N, 1024), _tile(K, 2048)
    nk = K // tk
    dims = {"nn": (((1,), (0,)), ((), ())), "nt": (((1,), (1,)), ((), ())), "tn": (((0,), (0,)), ((), ()))}[mode]
    n_ex, n_out = len(extras), len(out_shapes)

    def body(*refs):
        a_ref, b_ref = refs[:2]
        ex = refs[2:2 + n_ex]
        outs = refs[2 + n_ex:2 + n_ex + n_out]
        k = pl.program_id(2)

        def product():
            return lax.dot_general(a_ref[...].astype(BF16), b_ref[...].astype(BF16), dims,
                                   preferred_element_type=F32)

        def finish(total):
            res = (total,) if epilogue is None else epilogue(total, *[e[...] for e in ex])
            for o, r in zip(outs, res):
                o[...] = r.astype(o.dtype)

        if nk == 1:
            finish(product())
            return
        acc = refs[-1]

        @pl.when(k == 0)
        def _():
            acc[...] = product()

        @pl.when((k > 0) & (k < nk - 1))
        def _():
            acc[...] += product()

        @pl.when(k == nk - 1)
        def _():
            finish(acc[...] + product())

    a_spec = pl.BlockSpec((tk, tm), lambda i, j, k: (k, i)) if mode == "tn" else pl.BlockSpec((tm, tk), lambda i, j, k: (i, k))
    b_spec = pl.BlockSpec((tn, tk), lambda i, j, k: (j, k)) if mode == "nt" else pl.BlockSpec((tk, tn), lambda i, j, k: (k, j))
    mn = pl.BlockSpec((tm, tn), lambda i, j, k: (i, j))
    outs, moved = _call(
        body, (a, b, *extras), grid=(M // tm, N // tn, nk),
        in_specs=[a_spec, b_spec] + [mn] * n_ex, out_specs=[mn] * n_out,
        out_shape=[jax.ShapeDtypeStruct((M, N), dt) for dt in out_shapes],
        scratch_shapes=[pltpu.VMEM((tm, tn), F32)] if nk > 1 else [],
        sem=("parallel", "parallel", "arbitrary"), name=name, carried=carried)
    return list(outs) if carried is None else list(outs) + [moved]


def _rms_fwd(x, g, name):
    T, D = x.shape
    tr = _tile(T, 256)

    def body(x_ref, g_ref, h_ref):
        xv = x_ref[...]
        r = lax.rsqrt(jnp.mean(xv * xv, axis=-1, keepdims=True) + NORM_EPS)
        h_ref[...] = (xv * r * g_ref[...]).astype(BF16)

    return pl.pallas_call(
        body, grid=(T // tr,),
        in_specs=[pl.BlockSpec((tr, D), lambda i: (i, 0)), pl.BlockSpec((1, D), lambda i: (0, 0))],
        out_specs=pl.BlockSpec((tr, D), lambda i: (i, 0)),
        out_shape=jax.ShapeDtypeStruct((T, D), BF16), compiler_params=_cp("parallel"), name=name,
    )(x, g.reshape(1, D))


def _rms_bwd(dh, x, g, dres, name):
    T, D = x.shape
    tr = _tile(T, 256)

    def body(dh_ref, x_ref, g_ref, dres_ref, dx_ref, dxb_ref, dg_ref):
        @pl.when(pl.program_id(0) == 0)
        def _():
            dg_ref[...] = jnp.zeros_like(dg_ref)

        xv = x_ref[...]
        r = lax.rsqrt(jnp.mean(xv * xv, axis=-1, keepdims=True) + NORM_EPS)
        xhat = xv * r
        dhv = dh_ref[...].astype(F32)
        gd = dhv * g_ref[...]
        dx = dres_ref[...] + r * (gd - xhat * jnp.mean(gd * xhat, axis=-1, keepdims=True))
        dx_ref[...] = dx
        dxb_ref[...] = dx.astype(BF16)
        dg_ref[...] += jnp.sum(dhv * xhat, axis=0, keepdims=True)

    row = pl.BlockSpec((tr, D), lambda i: (i, 0))
    vec = pl.BlockSpec((1, D), lambda i: (0, 0))
    return pl.pallas_call(
        body, grid=(T // tr,), in_specs=[row, row, vec, row], out_specs=[row, row, vec],
        out_shape=[jax.ShapeDtypeStruct((T, D), F32), jax.ShapeDtypeStruct((T, D), BF16),
                   jax.ShapeDtypeStruct((1, D), F32)],
        compiler_params=_cp("arbitrary"), name=name,
    )(dh, x, g.reshape(1, D), dres)


def _loss_head(x, g, tgt):
    T, D = x.shape
    tr = _tile(T, 256)

    def body(x_ref, g_ref, t_ref, loss_ref, dx_ref, dxb_ref, dg_ref):
        @pl.when(pl.program_id(0) == 0)
        def _():
            dg_ref[...] = jnp.zeros_like(dg_ref)
            loss_ref[...] = jnp.zeros_like(loss_ref)

        xv = x_ref[...]
        r = lax.rsqrt(jnp.mean(xv * xv, axis=-1, keepdims=True) + NORM_EPS)
        xhat = xv * r
        err = xhat * g_ref[...] - t_ref[...]
        loss_ref[...] += (0.5 / D) * jnp.sum(jnp.sum(err * err, axis=-1, keepdims=True), axis=0, keepdims=True)
        dy = err * (1.0 / D)
        gd = dy * g_ref[...]
        dx = r * (gd - xhat * jnp.mean(gd * xhat, axis=-1, keepdims=True))
        dx_ref[...] = dx
        dxb_ref[...] = dx.astype(BF16)
        dg_ref[...] += jnp.sum(dy * xhat, axis=0, keepdims=True)

    row = pl.BlockSpec((tr, D), lambda i: (i, 0))
    vec = pl.BlockSpec((1, D), lambda i: (0, 0))
    return pl.pallas_call(
        body, grid=(T // tr,), in_specs=[row, vec, row],
        out_specs=[pl.BlockSpec((1, 128), lambda i: (0, 0)), row, row, vec],
        out_shape=[jax.ShapeDtypeStruct((1, 128), F32), jax.ShapeDtypeStruct((T, D), F32),
                   jax.ShapeDtypeStruct((T, D), BF16), jax.ShapeDtypeStruct((1, D), F32)],
        compiler_params=_cp("arbitrary"), name="loss_head",
    )(x, g.reshape(1, D), tgt)


def _sigmoid(x):
    return 1.0 / (1.0 + jnp.exp(-x))


def _gate_mix_fwd(proj, y_gdn, y_swa):
    T = proj.shape[0]
    tr, tc = _tile(T, 512), 512
    nc = D_MODEL // tc

    def body(gg_ref, gs_ref, yg_ref, ys_ref, mix_ref):
        mix_ref[...] = (_sigmoid(gg_ref[...]) * yg_ref[...] + _sigmoid(gs_ref[...]) * ys_ref[...]).astype(BF16)

    blk = pl.BlockSpec((tr, tc), lambda i, j: (i, j))
    return pl.pallas_call(
        body, grid=(T // tr, nc),
        in_specs=[pl.BlockSpec((tr, tc), lambda i, j: (i, GG0 // tc + j)),
                  pl.BlockSpec((tr, tc), lambda i, j: (i, GS0 // tc + j)), blk, blk],
        out_specs=blk, out_shape=jax.ShapeDtypeStruct((T, D_MODEL), BF16),
        compiler_params=_cp("parallel", "parallel"), name="gate_mix_fwd",
    )(proj, proj, y_gdn, y_swa)


def _gate_mix_bwd(dmix, proj, y_gdn, y_swa):
    T = proj.shape[0]
    tr = _tile(T, 128)
    gw = 2 * D_MODEL
    assert GG0 % gw == 0 and GS0 == GG0 + D_MODEL

    def body(dm_ref, gl_ref, yg_ref, ys_ref, dyg_ref, dys_ref, dproj_ref):
        dm = dm_ref[...]
        sg, ss = _sigmoid(gl_ref[:, :D_MODEL]), _sigmoid(gl_ref[:, D_MODEL:])
        dyg_ref[...] = (dm * sg).astype(BF16)
        dys_ref[...] = (dm * ss).astype(BF16)
        dproj_ref[:, :D_MODEL] = (dm * yg_ref[...] * sg * (1.0 - sg)).astype(BF16)
        dproj_ref[:, D_MODEL:] = (dm * ys_ref[...] * ss * (1.0 - ss)).astype(BF16)

    blk = pl.BlockSpec((tr, D_MODEL), lambda i: (i, 0))
    gates = pl.BlockSpec((tr, gw), lambda i: (i, GG0 // gw))
    out = jax.ShapeDtypeStruct((T, D_MODEL), BF16)
    return pl.pallas_call(
        body, grid=(T // tr,), in_specs=[blk, gates, blk, blk], out_specs=[blk, blk, gates],
        out_shape=[out, out, jax.ShapeDtypeStruct((T, N_PROJ), BF16)],
        compiler_params=_cp("parallel"), name="gate_mix_bwd",
    )(dmix, proj, y_gdn, y_swa)


CONV_TT, CONV_TW, HALO = 512, 256, 8


def _conv_fwd(proj, conv_w, carried=None):
    T = proj.shape[0]
    tt = _tile(T, CONV_TT)
    W = 3 * GDN_W

    def body(x_ref, halo_ref, w_ref, y_ref):
        first = pl.program_id(1) == 0
        halo = jnp.where(first, 0.0, halo_ref[...])
        xe = jnp.concatenate([halo, x_ref[...]], axis=0)
        acc = xe[HALO:] * w_ref[pl.ds(CONV_K - 1, 1), :]
        for j in range(CONV_K - 1):
            acc = acc + pltpu.roll(xe, CONV_K - 1 - j, 0)[HALO:] * w_ref[pl.ds(j, 1), :]
        y_ref[...] = acc

    (y,), moved = _call(
        body, (proj, proj, conv_w), grid=(W // CONV_TW, T // tt),
        in_specs=[pl.BlockSpec((tt, CONV_TW), lambda c, t: (t, c)),
                  pl.BlockSpec((HALO, CONV_TW), lambda c, t: (jnp.maximum(t * (tt // HALO) - 1, 0), c)),
                  pl.BlockSpec((CONV_K, CONV_TW), lambda c, t: (0, c))],
        out_specs=[pl.BlockSpec((tt, CONV_TW), lambda c, t: (t, c))],
        out_shape=[jax.ShapeDtypeStruct((T, W), F32)],
        sem=("parallel", "parallel"), name="conv_fwd", carried=carried)
    return y, moved


def _conv_bwd(dy, proj, conv_w, dproj, carried=None):
    T = proj.shape[0]
    tt = _tile(T, CONV_TT)
    nt = T // tt
    W = 3 * GDN_W

    def body(dy_ref, dnext_ref, x_ref, halo_ref, w_ref, _, dx_ref, dw_ref):
        t = pl.program_id(1)

        @pl.when(t == 0)
        def _():
            dw_ref[...] = jnp.zeros_like(dw_ref)

        dyv = dy_ref[...]
        dye = jnp.concatenate([dyv, jnp.where(t == nt - 1, 0.0, dnext_ref[...])], axis=0)
        xe = jnp.concatenate([jnp.where(t == 0, 0.0, halo_ref[...]), x_ref[...]], axis=0)
        acc = dyv * w_ref[pl.ds(CONV_K - 1, 1), :]
        dw_ref[pl.ds(CONV_K - 1, 1), :] += jnp.sum(dyv * xe[HALO:], axis=0, keepdims=True)
        for s in range(1, CONV_K):
            j = CONV_K - 1 - s
            acc = acc + pltpu.roll(dye, tt + HALO - s, 0)[:tt] * w_ref[pl.ds(j, 1), :]
            dw_ref[pl.ds(j, 1), :] += jnp.sum(dyv * pltpu.roll(xe, s, 0)[HALO:], axis=0, keepdims=True)
        dx_ref[...] = acc.astype(BF16)

    cur = pl.BlockSpec((tt, CONV_TW), lambda c, t: (t, c))
    per_plane = GDN_W // CONV_TW
    (dproj, dw), moved = _call(
        body, (dy, dy, proj, proj, conv_w, dproj), grid=(W // CONV_TW, nt),
        in_specs=[pl.BlockSpec((None, tt, CONV_TW), lambda c, t: (c // per_plane, t, c % per_plane)),
                  pl.BlockSpec((None, HALO, CONV_TW),
                               lambda c, t: (c // per_plane, jnp.minimum((t + 1) * (tt // HALO), T // HALO - 1),
                                             c % per_plane)),
                  cur,
                  pl.BlockSpec((HALO, CONV_TW), lambda c, t: (jnp.maximum(t * (tt // HALO) - 1, 0), c)),
                  pl.BlockSpec((CONV_K, CONV_TW), lambda c, t: (0, c)),
                  pl.BlockSpec(memory_space=pl.ANY)],
        out_specs=[cur, pl.BlockSpec((CONV_K, CONV_TW), lambda c, t: (0, c))],
        out_shape=[jax.ShapeDtypeStruct(dproj.shape, BF16), jax.ShapeDtypeStruct((CONV_K, W), F32)],
        sem=("parallel", "arbitrary"), name="conv_bwd", carried=carried, aliases={5: 0})
    return dproj, dw, moved


def _dot(a, b, kind, prec):
    nb = a.ndim - 2
    batch = tuple(range(nb))
    ca = nb if kind == "tn" else nb + 1
    cb = nb + 1 if kind == "nt" else nb
    dims = (((ca,), (cb,)), (batch, batch))
    if prec == "bf16":
        return lax.dot_general(a.astype(BF16), b.astype(BF16), dims, preferred_element_type=F32)
    return lax.dot_general(a, b, dims, precision=HI if prec == "f32" else lax.Precision.HIGH,
                           preferred_element_type=F32)


@functools.lru_cache(maxsize=None)
def _mm(kind, prec):
    @jax.custom_vjp
    def f(a, b):
        return _dot(a, b, kind, prec)

    def fwd(a, b):
        return f(a, b), (a, b)

    def bwd(res, ct):
        a, b = res
        if kind == "nn":
            return _dot(ct, b, "nt", prec), _dot(a, ct, "tn", prec)
        if kind == "nt":
            return _dot(ct, b, "nn", prec), _dot(ct, a, "tn", prec)
        return _dot(b, ct, "nt", prec), _dot(a, ct, "nn", prec)

    f.defvjp(fwd, bwd)
    return f


def _col_from_row(row, eye):
    return jnp.sum(jnp.where(eye, row, 0.0), axis=2, keepdims=True)


def _silu(x):
    return x / (1.0 + jnp.exp(-x))


def _gdn_chunk(yq, yk, yv, z, a_row, b_row, a_log, dt_bias, norm_g, state):
    h = yq.shape[0]
    ii = lax.broadcasted_iota(jnp.int32, (1, CHUNK, CHUNK), 1)
    jj = lax.broadcasted_iota(jnp.int32, (1, CHUNK, CHUNK), 2)
    eye = ii == jj
    qr, kr, v = _silu(yq), _silu(yk), _silu(yv)
    q = qr * lax.rsqrt(jnp.sum(qr * qr, axis=-1, keepdims=True) + NORM_EPS) * (HEAD_DIM ** -0.5)
    k = kr * lax.rsqrt(jnp.sum(kr * kr, axis=-1, keepdims=True) + NORM_EPS)
    beta_row = _sigmoid(b_row)
    xa = a_row + dt_bias
    g_row = -jnp.exp(a_log) * (jnp.maximum(xa, 0.0) + jnp.log(1.0 + jnp.exp(-jnp.abs(xa))))
    upper = jnp.broadcast_to(jnp.where(ii <= jj, 1.0, 0.0).astype(F32), (h, CHUNK, CHUNK))
    decay_row = _mm("nn", "f32")(g_row, upper)
    decay_col = _col_from_row(decay_row, eye)
    beta_col = _col_from_row(beta_row, eye)
    decay_last = jnp.sum(g_row, axis=2, keepdims=True)
    gamma = jnp.exp(jnp.where(ii >= jj, decay_col - decay_row, -jnp.inf))
    k_beta = k * beta_col
    a_low = jnp.where(ii > jj, _mm("nt", GDN_PREC_SOLVE)(k_beta, k) * gamma, 0.0)
    t_inv = jnp.where(eye, 1.0, 0.0).astype(F32) - a_low
    p = a_low
    for _ in range(5):
        p = _mm("nn", GDN_PREC_SOLVE)(p, p)
        t_inv = t_inv + _mm("nn", GDN_PREC_SOLVE)(t_inv, p)
    e_dec = jnp.exp(decay_col)
    u = _mm("nn", GDN_PREC_SOLVE)(t_inv, v * beta_col)
    w = _mm("nn", GDN_PREC_SOLVE)(t_inv, k_beta * e_dec)
    qk = _mm("nt", GDN_PREC)(q, k) * gamma
    v_new = u - _mm("nn", GDN_PREC)(w, state)
    o = _mm("nn", GDN_PREC)(q * e_dec, state) + _mm("nn", GDN_PREC)(qk, v_new)
    s_new = state * jnp.exp(decay_last) + _mm("tn", GDN_PREC)(k * jnp.exp(decay_last - decay_col), v_new)
    o_n = o * lax.rsqrt(jnp.mean(o * o, axis=-1, keepdims=True) + NORM_EPS) * norm_g
    return o_n * _silu(z), s_new


def _heads(ref, hb):
    return jnp.stack([ref[:, HEAD_DIM * i:HEAD_DIM * (i + 1)] for i in range(hb)], axis=0)


def _gdn_fwd(y, proj, a_rows, b_rows, a_log, dt_bias, norm_g, carried=None):
    T = y.shape[0]
    H, HB = GDN_HEADS, GDN_HB
    NC, HG, BW = T // CHUNK, GDN_HEADS // GDN_HB, GDN_HB * HEAD_DIM

    def body(q_ref, k_ref, v_ref, z_ref, a_ref, b_ref, alog_ref, dt_ref, ng_ref, o_ref, sst_ref, s_scr):
        @pl.when(pl.program_id(1) == 0)
        def _():
            s_scr[...] = jnp.zeros_like(s_scr)

        state = s_scr[...]
        sst_ref[...] = state
        o_g, s_new = _gdn_chunk(_heads(q_ref, HB), _heads(k_ref, HB), _heads(v_ref, HB), _heads(z_ref, HB),
                                a_ref[...], b_ref[...], alog_ref[...], dt_ref[...], ng_ref[...], state)
        s_scr[...] = s_new
        for i in range(HB):
            o_ref[:, HEAD_DIM * i:HEAD_DIM * (i + 1)] = o_g[i].astype(BF16)

    col = lambda off: pl.BlockSpec((CHUNK, BW), lambda hg, n, off=off: (n, off + hg))
    row = pl.BlockSpec((HB, None, 1, CHUNK), lambda hg, n: (hg, n, 0, 0))
    sc = pl.BlockSpec((HB, 1, 1), lambda hg, n: (hg, 0, 0))
    (o, states), moved = _call(
        body, (y, y, y, proj, a_rows, b_rows, a_log, dt_bias, norm_g), grid=(HG, NC),
        in_specs=[col(0), col(HG), col(2 * HG), col(Z0 // BW), row, row, sc, sc,
                  pl.BlockSpec((1, 1, HEAD_DIM), lambda hg, n: (0, 0, 0))],
        out_specs=[col(0), pl.BlockSpec((HB, None, HEAD_DIM, HEAD_DIM), lambda hg, n: (hg, n, 0, 0))],
        out_shape=[jax.ShapeDtypeStruct((T, GDN_W), BF16), jax.ShapeDtypeStruct((H, NC, HEAD_DIM, HEAD_DIM), F32)],
        scratch_shapes=[pltpu.VMEM((HB, HEAD_DIM, HEAD_DIM), F32)],
        sem=("parallel", "arbitrary"), name="gdn_fwd", carried=carried)
    return o, states, moved


def _gdn_bwd(y, proj, a_rows, b_rows, a_log, dt_bias, norm_g, states, do, dproj, carried=None):
    T = y.shape[0]
    H, HB = GDN_HEADS, GDN_HB
    NC, HG, BW = T // CHUNK, GDN_HEADS // GDN_HB, GDN_HB * HEAD_DIM

    def body(q_ref, k_ref, v_ref, z_ref, a_ref, b_ref, alog_ref, dt_ref, ng_ref, sst_ref, do_ref, _,
             dy_ref, dz_ref, da_ref, db_ref, dalog_ref, ddt_ref, dng_ref, ds_scr):
        hg, n = pl.program_id(0), pl.program_id(1)

        @pl.when(n == 0)
        def _():
            ds_scr[...] = jnp.zeros_like(ds_scr)
            dalog_ref[...] = jnp.zeros_like(dalog_ref)
            ddt_ref[...] = jnp.zeros_like(ddt_ref)

        @pl.when((n == 0) & (hg == 0))
        def _():
            dng_ref[...] = jnp.zeros_like(dng_ref)

        args = (_heads(q_ref, HB), _heads(k_ref, HB), _heads(v_ref, HB), _heads(z_ref, HB), a_ref[...], b_ref[...],
                alog_ref[...], dt_ref[...], ng_ref[...], sst_ref[...])
        _, vjp = jax.vjp(_gdn_chunk, *args)
        dq, dk, dv, dz, da, db, dalog, ddt, dng, d_state = vjp((_heads(do_ref, HB).astype(F32), ds_scr[...]))
        ds_scr[...] = d_state
        for i in range(HB):
            sl = slice(HEAD_DIM * i, HEAD_DIM * (i + 1))
            dy_ref[0, :, sl] = dq[i]
            dy_ref[1, :, sl] = dk[i]
            dy_ref[2, :, sl] = dv[i]
            dz_ref[:, sl] = dz[i].astype(BF16)
        da_ref[...] = da
        db_ref[...] = db
        dalog_ref[...] += dalog
        ddt_ref[...] += ddt
        dng_ref[...] += dng

    rev = lambda n: NC - 1 - n
    col = lambda off: pl.BlockSpec((CHUNK, BW), lambda hg, n, off=off: (rev(n), off + hg))
    row = pl.BlockSpec((HB, None, 1, CHUNK), lambda hg, n: (hg, rev(n), 0, 0))
    sc = pl.BlockSpec((HB, 1, 1), lambda hg, n: (hg, 0, 0))
    ng = pl.BlockSpec((1, 1, HEAD_DIM), lambda hg, n: (0, 0, 0))
    sst = pl.BlockSpec((HB, None, HEAD_DIM, HEAD_DIM), lambda hg, n: (hg, rev(n), 0, 0))
    rw = jax.ShapeDtypeStruct((H, NC, 1, CHUNK), F32)
    s1 = jax.ShapeDtypeStruct((H, 1, 1), F32)
    (dy, dproj, da, db, dalog, ddt, dng), moved = _call(
        body, (y, y, y, proj, a_rows, b_rows, a_log, dt_bias, norm_g, states, do, dproj), grid=(HG, NC),
        in_specs=[col(0), col(HG), col(2 * HG), col(Z0 // BW), row, row, sc, sc, ng, sst, col(0),
                  pl.BlockSpec(memory_space=pl.ANY)],
        out_specs=[pl.BlockSpec((3, CHUNK, BW), lambda hg, n: (0, rev(n), hg)), col(Z0 // BW), row, row, sc, sc, ng],
        out_shape=[jax.ShapeDtypeStruct((3, T, GDN_W), F32), jax.ShapeDtypeStruct(dproj.shape, BF16), rw, rw, s1, s1,
                   jax.ShapeDtypeStruct((1, 1, HEAD_DIM), F32)],
        scratch_shapes=[pltpu.VMEM((HB, HEAD_DIM, HEAD_DIM), F32)],
        sem=("arbitrary", "arbitrary"), name="gdn_bwd", carried=carried, aliases={11: 1})
    return dy, dproj, da, db, dalog, ddt, dng, moved


def _swa_block(q, k_prev, k_cur, v_prev, v_cur, sink, slope, has_prev):
    kb = jnp.concatenate([k_prev, k_cur], axis=0)
    vb = jnp.concatenate([v_prev, v_cur], axis=0)
    q2 = q.reshape(SWA_GROUP * WINDOW, SWA_DIM)
    s = _mm("nt", SWA_PREC)(q2, kb)
    s = s.reshape(SWA_GROUP, WINDOW, 2 * WINDOW) * (SWA_DIM ** -0.5)
    qi = lax.broadcasted_iota(jnp.int32, (1, WINDOW, 2 * WINDOW), 1)
    sj = lax.broadcasted_iota(jnp.int32, (1, WINDOW, 2 * WINDOW), 2)
    dist = qi + WINDOW - sj
    valid = (dist >= 0) & (dist < WINDOW) & (has_prev | (sj >= WINDOW))
    s = jnp.where(valid, s - slope * dist.astype(F32), -jnp.inf)
    m = lax.stop_gradient(jnp.maximum(jnp.max(s, axis=-1, keepdims=True), sink))
    p = jnp.exp(s - m)
    probs = p / (jnp.sum(p, axis=-1, keepdims=True) + jnp.exp(sink - m))
    o = _mm("nn", SWA_PREC)(probs.reshape(SWA_GROUP * WINDOW, 2 * WINDOW), vb)
    return o.reshape(SWA_GROUP, WINDOW, SWA_DIM)


def _alibi_slopes():
    return (2.0 ** (-8.0 * jnp.arange(1, SWA_HEADS + 1, dtype=F32) / SWA_HEADS)).reshape(SWA_HEADS, 1, 1)


def _swa_fwd(q, k, v, sinks, carried=None):
    T = q.shape[1]
    NB = T // WINDOW

    KS, G = SWA_KV_STEP, SWA_GROUP

    def body(q_ref, kp_ref, kc_ref, vp_ref, vc_ref, sink_ref, slope_ref, o_ref):
        for i in range(KS):
            hq = slice(G * i, G * (i + 1))
            o = _swa_block(q_ref[hq], kp_ref[i], kc_ref[i], vp_ref[i], vc_ref[i], sink_ref[hq], slope_ref[hq],
                           pl.program_id(1) > 0)
            o_ref[hq] = o.astype(BF16)

    qs = pl.BlockSpec((KS * G, WINDOW, SWA_DIM), lambda h, n: (h, n, 0))
    cur = pl.BlockSpec((KS, WINDOW, SWA_DIM), lambda h, n: (h, n, 0))
    prev = pl.BlockSpec((KS, WINDOW, SWA_DIM), lambda h, n: (h, jnp.maximum(n - 1, 0), 0))
    hs = pl.BlockSpec((KS * G, 1, 1), lambda h, n: (h, 0, 0))
    (o,), moved = _call(
        body, (q, k, k, v, v, sinks, _alibi_slopes()), grid=(SWA_KV // KS, NB),
        in_specs=[qs, prev, cur, prev, cur, hs, hs], out_specs=[qs],
        out_shape=[jax.ShapeDtypeStruct((SWA_HEADS, T, SWA_DIM), BF16)],
        sem=("parallel", "parallel"), name="swa_fwd", carried=carried)
    return o, moved


def _swa_bwd(q, k, v, sinks, do, carried=None):
    T = q.shape[1]
    NB = T // WINDOW

    def body(q_ref, kp_ref, kc_ref, vp_ref, vc_ref, sink_ref, slope_ref, do_ref,
             dq_ref, dk_ref, dv_ref, dsink_ref, dk_scr, dv_scr):
        n = pl.program_id(1)

        @pl.when(n == 0)
        def _():
            dk_scr[...] = jnp.zeros_like(dk_scr)
            dv_scr[...] = jnp.zeros_like(dv_scr)
            dsink_ref[...] = jnp.zeros_like(dsink_ref)

        has_prev = n < NB - 1
        for i in range(KS):
            hq = slice(G * i, G * (i + 1))
            fn = functools.partial(_swa_block, slope=slope_ref[hq], has_prev=has_prev)
            _, vjp = jax.vjp(fn, q_ref[hq], kp_ref[i], kc_ref[i], vp_ref[i], vc_ref[i], sink_ref[hq])
            dq, dkp, dkc, dvp, dvc, dsink = vjp(do_ref[hq].astype(F32))
            dq_ref[hq] = dq.astype(BF16)
            dk_ref[i] = (dkc + dk_scr[i]).astype(BF16)
            dv_ref[i] = (dvc + dv_scr[i]).astype(BF16)
            dk_scr[i] = dkp
            dv_scr[i] = dvp
            dsink_ref[hq] += dsink

    KS, G = SWA_KV_STEP, SWA_GROUP
    rev = lambda n: NB - 1 - n
    qs = pl.BlockSpec((KS * G, WINDOW, SWA_DIM), lambda h, n: (h, rev(n), 0))
    cur = pl.BlockSpec((KS, WINDOW, SWA_DIM), lambda h, n: (h, rev(n), 0))
    prev = pl.BlockSpec((KS, WINDOW, SWA_DIM), lambda h, n: (h, jnp.maximum(rev(n) - 1, 0), 0))
    hs = pl.BlockSpec((KS * G, 1, 1), lambda h, n: (h, 0, 0))
    kv = jax.ShapeDtypeStruct((SWA_KV, T, SWA_DIM), BF16)
    (dq, dk, dv, dsink), moved = _call(
        body, (q, k, k, v, v, sinks, _alibi_slopes(), do), grid=(SWA_KV // KS, NB),
        in_specs=[qs, prev, cur, prev, cur, hs, hs, qs], out_specs=[qs, cur, cur, hs],
        out_shape=[jax.ShapeDtypeStruct((SWA_HEADS, T, SWA_DIM), BF16), kv, kv,
                   jax.ShapeDtypeStruct((SWA_HEADS, 1, 1), F32)],
        scratch_shapes=[pltpu.VMEM((KS, WINDOW, SWA_DIM), F32), pltpu.VMEM((KS, WINDOW, SWA_DIM), F32)],
        sem=("parallel", "arbitrary"), name="swa_bwd", carried=carried)
    return dq, dk, dv, dsink, moved


def _to_heads(t, n_heads):
    return t.reshape(t.shape[0], n_heads, SWA_DIM).transpose(1, 0, 2)


def _from_heads(t):
    return t.transpose(1, 0, 2).reshape(t.shape[1], -1)


def _rows(t):
    return t.T.reshape(t.shape[1], t.shape[0] // CHUNK, 1, CHUNK)


def _unrows(t):
    return t.reshape(t.shape[0], -1).T


def _branch_inputs(proj):
    b_rows = _rows(proj[:, BA0:BA0 + GDN_HEADS])
    a_rows = _rows(proj[:, BA0 + GDN_HEADS:BA0 + 2 * GDN_HEADS])
    q_s = _to_heads(proj[:, QS0:QS0 + D_MODEL], SWA_HEADS)
    k_s = _to_heads(proj[:, KS0:KS0 + KV_W], SWA_KV)
    v_s = _to_heads(proj[:, VS0:VS0 + KV_W], SWA_KV)
    return a_rows, b_rows, q_s, k_s, v_s


def _relu2_epilogue(acc):
    r = jnp.maximum(acc, 0.0)
    return acc, r * r


_BRANCH_OUT = ("w_branch_gdn", "w_branch_swa", "w_out")
_FIRST_NEEDED = ("w_in", "conv_w")


def _gather_plan(l):
    nxt = l + 1
    if l == 0:
        plan = {"mm_proj": [(0, k) for k in _BRANCH_OUT + ("w_ff_up",)], "conv_fwd": [(0, "w_ff_down")],
                "gdn_fwd": [(nxt, k) for k in _FIRST_NEEDED], "swa_fwd": [(nxt, k) for k in _BRANCH_OUT],
                "mm_up": [(nxt, "w_ff_up")], "mm_down": [(nxt, "w_ff_down")]}
    elif nxt < DEPTH:
        plan = {"mm_proj": [(nxt, k) for k in _FIRST_NEEDED], "gdn_fwd": [(nxt, k) for k in _BRANCH_OUT],
                "swa_fwd": [(nxt, "w_ff_up")], "mm_up": [(nxt, "w_ff_down")]}
    else:
        plan = {}
    return plan


def _scatter_plan(l):
    if l == DEPTH - 1:
        return {"swa_bwd": ["w_ff_up"], "gdn_bwd": ["w_ff_down"] + list(_BRANCH_OUT)}
    plan = {"swa_bwd": ["w_ff_up"], "gdn_bwd": ["late", "w_ff_down"], "mm_proj_dw": list(_BRANCH_OUT)}
    if l == 0:
        plan["mm_proj_dx"] = ["own_late"]
    return plan


class _Weights:
    def __init__(self, params):
        self.params, self.full = params, {}

    def put(self, layer, name, gathered):
        self.full[layer, name] = _align_w_in(gathered) if name == "w_in" else _unblock(gathered, _SHARDED[name])

    def get(self, layer, name):
        return self.full[layer, name] if name in _SHARDED else self.params[name][layer]


def _layer_fwd(x, l, weights, shards):
    plan = _gather_plan(l)
    w = functools.partial(weights.get, l)

    def carry(host):
        items = plan.get(host)
        return _Exchange("gather", [shards[i][k] for i, k in items]) if items else None

    def keep(host, moved):
        for (i, k), gathered in zip(plan.get(host, ()), moved or ()):
            weights.put(i, k, gathered)

    h1 = _rms_fwd(x, w("norm1_g"), "rms1_fwd")
    proj, *moved = _matmul(h1, w("w_in"), "nn", [F32], name="mm_proj", carried=carry("mm_proj"))
    keep("mm_proj", moved[0] if moved else None)
    a_rows, b_rows, q_s, k_s, v_s = _branch_inputs(proj)
    y, moved = _conv_fwd(proj, w("conv_w"), carried=carry("conv_fwd"))
    keep("conv_fwd", moved)
    o_gdn, states, moved = _gdn_fwd(y, proj, a_rows, b_rows, w("a_log").reshape(-1, 1, 1),
                                    w("dt_bias").reshape(-1, 1, 1), w("gdn_norm_g").reshape(1, 1, -1),
                                    carried=carry("gdn_fwd"))
    keep("gdn_fwd", moved)
    o_swa, moved = _swa_fwd(q_s, k_s, v_s, w("attn_sinks").reshape(-1, 1, 1), carried=carry("swa_fwd"))
    keep("swa_fwd", moved)
    o_swa = _from_heads(o_swa)
    y_gdn, = _matmul(o_gdn, w("w_branch_gdn"), "nn", [F32], name="mm_bgdn")
    y_swa, = _matmul(o_swa, w("w_branch_swa"), "nn", [F32], name="mm_bswa")
    mix = _gate_mix_fwd(proj, y_gdn, y_swa)
    x2, = _matmul(mix, w("w_out"), "nn", [F32], epilogue=lambda acc, r: (r + acc,), extras=(x,), name="mm_out")
    h2 = _rms_fwd(x2, w("norm2_g"), "rms2_fwd")
    u, act, *moved = _matmul(h2, w("w_ff_up"), "nn", [F32, BF16], epilogue=_relu2_epilogue, name="mm_up",
                             carried=carry("mm_up"))
    keep("mm_up", moved[0] if moved else None)
    x3, *moved = _matmul(act, w("w_ff_down"), "nn", [F32], epilogue=lambda acc, r: (r + acc,), extras=(x2,),
                         name="mm_down", carried=carry("mm_down"))
    keep("mm_down", moved[0] if moved else None)
    saved = dict(x=x, h1=h1, proj=proj, y=y, states=states, o_gdn=o_gdn, o_swa=o_swa, y_gdn=y_gdn, y_swa=y_swa,
                 mix=mix, x2=x2, h2=h2, u=u, act=act)
    return x3, saved


def _layer_bwd(dx3, dx3b, l, weights, s, late, landed):
    T = dx3.shape[0]
    g = {}
    plan = _scatter_plan(l)
    w = functools.partial(weights.get, l)

    own_late = []

    def carry(host):
        arrays = []
        for k in plan.get(host, ()):
            arrays += late if k == "late" else own_late if k == "own_late" else [_block(g[k], _SHARDED[k])]
        return _Exchange("scatter", arrays) if arrays else None

    def keep(host, moved):
        names = []
        for k in plan.get(host, ()):
            names += ([(l + 1, n) for n in _FIRST_NEEDED] if k == "late" else
                      [(l, n) for n in _FIRST_NEEDED] if k == "own_late" else [(l, k)])
        for (i, k), blocks in zip(names, moved or ()):
            landed[i][k] = blocks

    du, = _matmul(dx3b, w("w_ff_down"), "nt", [BF16], extras=(s["u"],),
                  epilogue=lambda acc, uu: (acc * 2.0 * jnp.maximum(uu, 0.0),), name="mm_down_dx")
    g["w_ff_down"], = _matmul(s["act"], dx3b, "tn", [BF16], name="mm_down_dw")
    dh2, = _matmul(du, w("w_ff_up"), "nt", [F32], name="mm_up_dx")
    g["w_ff_up"], = _matmul(s["h2"], du, "tn", [BF16], name="mm_up_dw")
    dx2, dx2b, g["norm2_g"] = _rms_bwd(dh2, s["x2"], w("norm2_g"), dx3, "rms2_bwd")
    dmix, = _matmul(dx2b, w("w_out"), "nt", [F32], name="mm_out_dx")
    g["w_out"], = _matmul(s["mix"], dx2b, "tn", [BF16], name="mm_out_dw")
    dy_gdn, dy_swa, dproj = _gate_mix_bwd(dmix, s["proj"], s["y_gdn"], s["y_swa"])
    do_gdn, = _matmul(dy_gdn, w("w_branch_gdn"), "nt", [BF16], name="mm_bgdn_dx")
    g["w_branch_gdn"], = _matmul(s["o_gdn"], dy_gdn, "tn", [BF16], name="mm_bgdn_dw")
    do_swa, = _matmul(dy_swa, w("w_branch_swa"), "nt", [BF16], name="mm_bswa_dx")
    g["w_branch_swa"], = _matmul(s["o_swa"], dy_swa, "tn", [BF16], name="mm_bswa_dw")
    a_rows, b_rows, q_s, k_s, v_s = _branch_inputs(s["proj"])
    dq_s, dk_s, dv_s, dsink, moved = _swa_bwd(q_s, k_s, v_s, w("attn_sinks").reshape(-1, 1, 1),
                                              _to_heads(do_swa, SWA_HEADS), carried=carry("swa_bwd"))
    keep("swa_bwd", moved)
    g["attn_sinks"] = dsink.reshape(-1)
    dy, dproj, da_rows, db_rows, dalog, ddt, dng, moved = _gdn_bwd(
        s["y"], s["proj"], a_rows, b_rows, w("a_log").reshape(-1, 1, 1), w("dt_bias").reshape(-1, 1, 1),
        w("gdn_norm_g").reshape(1, 1, -1), s["states"], do_gdn, dproj, carried=carry("gdn_bwd"))
    keep("gdn_bwd", moved)
    g["a_log"], g["dt_bias"], g["gdn_norm_g"] = dalog.reshape(-1), ddt.reshape(-1), dng.reshape(-1)
    dproj, g["conv_w"], _ = _conv_bwd(dy, s["proj"], w("conv_w"), dproj)
    dba = jnp.concatenate([_unrows(db_rows), _unrows(da_rows)], axis=1).astype(BF16)
    tail = jnp.concatenate([_from_heads(dq_s), _from_heads(dk_s), _from_heads(dv_s), dba,
                            jnp.zeros((T, N_PROJ - BA0 - 2 * GDN_HEADS), BF16)], axis=1)
    dproj = lax.dynamic_update_slice(dproj, tail, (0, QS0))
    g["w_in"], *moved = _matmul(s["h1"], dproj, "tn", [BF16], name="mm_proj_dw", carried=carry("mm_proj_dw"))
    keep("mm_proj_dw", moved[0] if moved else None)
    own_late += [_block_w_in(g["w_in"]), _block(g["conv_w"], _SHARDED["conv_w"]).astype(BF16)]
    dh1, *moved = _matmul(dproj, w("w_in"), "nt", [F32], name="mm_proj_dx", carried=carry("mm_proj_dx"))
    keep("mm_proj_dx", moved[0] if moved else None)
    dx, dxb, g["norm1_g"] = _rms_bwd(dh1, s["x"], w("norm1_g"), dx2, "rms1_bwd")
    g["norm1_g"], g["norm2_g"] = g["norm1_g"].reshape(-1), g["norm2_g"].reshape(-1)
    return dx, dxb, g, own_late


def _all_reduce_small(part):
    R = part.shape[0]

    def body(x_ref, sum_ref, gath, send_sems, recv_sems):
        me = _my_index()
        gath[me] = x_ref[...]
        sends = []
        for mask in range(1, N_DEV):
            dev, _ = _peer(mask)
            sends.append(pltpu.make_async_remote_copy(
                src_ref=x_ref, dst_ref=gath.at[me], send_sem=send_sems.at[mask - 1], recv_sem=recv_sems.at[mask - 1],
                device_id=dev, device_id_type=MESH))
        for cp in sends:
            cp.start()
        for mask in range(1, N_DEV):
            dev, idx = _peer(mask)
            pltpu.make_async_remote_copy(
                src_ref=x_ref, dst_ref=gath.at[idx], send_sem=send_sems.at[mask - 1], recv_sem=recv_sems.at[mask - 1],
                device_id=dev, device_id_type=MESH).wait_recv()
        for cp in sends:
            cp.wait_send()
        acc = gath[0]
        for i in range(1, N_DEV):
            acc = acc + gath[i]
        sum_ref[...] = acc

    vm = pl.BlockSpec(memory_space=pltpu.VMEM)
    return pl.pallas_call(
        body, in_specs=[vm], out_specs=vm, out_shape=jax.ShapeDtypeStruct((R, 128), F32),
        scratch_shapes=[pltpu.VMEM((N_DEV, R, 128), F32), pltpu.SemaphoreType.DMA((7,)), pltpu.SemaphoreType.DMA((7,))],
        name="small_all_reduce",
    )(part)


def _adam_math(w, g, m, v):
    m = ADAM_B1 * m + (1.0 - ADAM_B1) * g
    v = ADAM_B2 * v + (1.0 - ADAM_B2) * (g * g)
    m_hat = m / (1.0 - ADAM_B1 ** ADAM_STEP)
    v_hat = v / (1.0 - ADAM_B2 ** ADAM_STEP)
    delta = -ADAM_LR * (m_hat / (jnp.sqrt(v_hat) + ADAM_EPS) + ADAM_WD * w)
    return delta, m, v


def _adamw_sum(parts, w, m, v, name):
    n_layers = len(parts)
    R, Cc = parts[0].shape[1:]
    tr = R
    while tr * Cc * 4 > (1 << 20) and tr % 32 == 0:
        tr //= 2
    nblk = R // tr

    def body(*refs):
        p_refs = refs[:n_layers]
        w_ref, m_ref, v_ref, g_ref, d_ref, nm_ref, nv_ref = refs[n_layers:]
        for j in range(n_layers):
            @pl.when(pl.program_id(0) == j)
            def _(p_ref=p_refs[j]):
                g = p_ref[0].astype(F32)
                for i in range(1, N_DEV):
                    g = g + p_ref[i].astype(F32)
                d, nm, nv = _adam_math(w_ref[...], g, m_ref[...], v_ref[...])
                g_ref[...], d_ref[...], nm_ref[...], nv_ref[...] = g, d, nm, nv

    def part_spec(j):
        return pl.BlockSpec((N_DEV, tr, Cc), lambda l, i: (0, jnp.where(l == j, i, jnp.where(l < j, 0, nblk - 1)), 0))

    blk = pl.BlockSpec((tr, Cc), lambda l, i: (l * nblk + i, 0))
    out = jax.ShapeDtypeStruct((n_layers * R, Cc), F32)
    return pl.pallas_call(
        body, grid=(n_layers, nblk), in_specs=[part_spec(j) for j in range(n_layers)] + [blk, blk, blk],
        out_specs=[blk] * 4, out_shape=[out] * 4, compiler_params=_cp("arbitrary", "arbitrary"), name=name,
    )(*parts, w, m, v)


def _adamw_small(g, w, m, v):
    def body(g_ref, w_ref, m_ref, v_ref, d_ref, nm_ref, nv_ref):
        d_ref[...], nm_ref[...], nv_ref[...] = _adam_math(w_ref[...], g_ref[...], m_ref[...], v_ref[...])

    out = jax.ShapeDtypeStruct(g.shape, F32)
    return pl.pallas_call(body, out_shape=[out] * 3, name="adamw_small")(g, w, m, v)


def _w_in_pieces():
    per = sum(width for _, width, _ in _SEGS) // N_DEV
    pieces = []
    for src, width, dst in _SEGS:
        c = src
        while c < src + width:
            d = c // per
            n = min(src + width, (d + 1) * per) - c
            pieces.append((d, c - d * per, dst + c - src, n))
            c += n
    return pieces


def _align_w_in(gathered):
    pieces = sorted(_w_in_pieces(), key=lambda p: p[2])
    end = pieces[-1][2] + pieces[-1][3]
    parts = [gathered[d, :, a:a + n] for d, a, _, n in pieces]
    return jnp.concatenate(parts + [jnp.zeros((gathered.shape[1], N_PROJ - end), gathered.dtype)], axis=1)


def _block_w_in(g):
    pieces = sorted(_w_in_pieces(), key=lambda p: (p[0], p[1]))
    return jnp.stack([jnp.concatenate([g[:, dst:dst + n] for d, _, dst, n in pieces if d == dev], axis=1)
                      for dev in range(N_DEV)])


_SHARDED = {"w_in": 1, "conv_w": 1, "w_branch_gdn": 0, "w_branch_swa": 0, "w_out": 0, "w_ff_up": 1, "w_ff_down": 0}
_SMALL = ("norm1_g", "a_log", "dt_bias", "gdn_norm_g", "attn_sinks", "norm2_g", "final_norm_g")


def _unblock(gathered, axis):
    if axis == 0:
        return gathered.reshape(-1, gathered.shape[2])
    return gathered.transpose(1, 0, 2).reshape(gathered.shape[1], -1)


def _block(full, axis):
    A, B = full.shape
    if axis == 0:
        return full.reshape(N_DEV, A // N_DEV, B)
    return full.reshape(A, N_DEV, B // N_DEV).transpose(1, 0, 2)


def _pack_small(d, loss_row=None):
    rows = [d[k].astype(F32).reshape(-1, 128) if d[k].size % 128 == 0 else
            jnp.pad(d[k].astype(F32), ((0, 0), (0, 128 - d[k].shape[-1]))) for k in _SMALL]
    rows.append(jnp.zeros((1, 128), F32) if loss_row is None else loss_row)
    packed = jnp.concatenate(rows, axis=0)
    return jnp.pad(packed, ((0, -packed.shape[0] % 8), (0, 0)))


def _unpack_small(packed, like):
    out, r = {}, 0
    for k in _SMALL:
        shp = like[k].shape
        if like[k].size % 128 == 0:
            n = like[k].size // 128
            out[k] = packed[r:r + n].reshape(shp)
        else:
            n = shp[0]
            out[k] = packed[r:r + n, :shp[-1]]
        r += n
    return out, packed[r, 0]


def kernel(x, norm1_g, w_in, conv_w, a_log, dt_bias, gdn_norm_g, attn_sinks, w_branch_gdn, w_branch_swa, w_out, norm2_g, w_ff_up, w_ff_down, final_norm_g, loss_target, m_norm1_g, m_w_in, m_conv_w, m_a_log, m_dt_bias, m_gdn_norm_g, m_attn_sinks, m_w_branch_gdn, m_w_branch_swa, m_w_out, m_norm2_g, m_w_ff_up, m_w_ff_down, m_final_norm_g, v_norm1_g, v_w_in, v_conv_w, v_a_log, v_dt_bias, v_gdn_norm_g, v_attn_sinks, v_w_branch_gdn, v_w_branch_swa, v_w_out, v_norm2_g, v_w_ff_up, v_w_ff_down, v_final_norm_g):
    names = ("norm1_g", "w_in", "conv_w", "a_log", "dt_bias", "gdn_norm_g", "attn_sinks", "w_branch_gdn",
             "w_branch_swa", "w_out", "norm2_g", "w_ff_up", "w_ff_down", "final_norm_g")
    w = dict(zip(names, (norm1_g, w_in, conv_w, a_log, dt_bias, gdn_norm_g, attn_sinks, w_branch_gdn, w_branch_swa,
                         w_out, norm2_g, w_ff_up, w_ff_down, final_norm_g)))
    m = dict(zip(names, (m_norm1_g, m_w_in, m_conv_w, m_a_log, m_dt_bias, m_gdn_norm_g, m_attn_sinks, m_w_branch_gdn,
                         m_w_branch_swa, m_w_out, m_norm2_g, m_w_ff_up, m_w_ff_down, m_final_norm_g)))
    v = dict(zip(names, (v_norm1_g, v_w_in, v_conv_w, v_a_log, v_dt_bias, v_gdn_norm_g, v_attn_sinks, v_w_branch_gdn,
                         v_w_branch_swa, v_w_out, v_norm2_g, v_w_ff_up, v_w_ff_down, v_final_norm_g)))
    sharded = list(_SHARDED)

    shards = [{k: w[k][l] if k == "conv_w" else w[k][l].astype(BF16) for k in sharded} for l in range(DEPTH)]

    weights = _Weights(w)
    for k, gathered in zip(_FIRST_NEEDED, _exchange("gather", [shards[0][k] for k in _FIRST_NEEDED], "weight_all_gather")):
        weights.put(0, k, gathered)
    xc, saved = x[0], []
    for l in range(DEPTH):
        xc, s = _layer_fwd(xc, l, weights, shards)
        saved.append(s)
    loss_row, dx, dxb, dgf = _loss_head(xc, final_norm_g, loss_target[0])

    grads, landed, late = [None] * DEPTH, [{} for _ in range(DEPTH)], None
    for l in reversed(range(DEPTH)):
        dx, dxb, grads[l], late = _layer_bwd(dx, dxb, l, weights, saved[l], late, landed)
    grad_x = dx
    out_g, out_d, out_m, out_v = {}, {}, {}, {}
    for k in sharded:
        shp = w[k].shape
        flat = lambda t: t.reshape(-1, shp[-1])
        parts = [landed[l][k] for l in range(DEPTH)]
        if parts[0].shape[1] % 8:
            parts = [jnp.concatenate(parts, axis=1)]
        res = _adamw_sum(parts, flat(w[k]), flat(m[k]), flat(v[k]), "adamw_" + k)
        out_g[k], out_d[k], out_m[k], out_v[k] = [t.reshape(shp) for t in res]

    small_g = {k: jnp.stack([gl[k] for gl in grads]) for k in _SMALL if k != "final_norm_g"}
    small_g["final_norm_g"] = dgf.reshape(-1)
    total = _all_reduce_small(_pack_small(small_g, loss_row))
    sd, sm, sv = _adamw_small(total, _pack_small(w), _pack_small(m), _pack_small(v))
    g_small, loss = _unpack_small(total, w)
    d_small, _ = _unpack_small(sd, w)
    m_small, _ = _unpack_small(sm, w)
    v_small, _ = _unpack_small(sv, w)
    out_g.update(g_small), out_d.update(d_small), out_m.update(m_small), out_v.update(v_small)

    return (loss, grad_x[None], *[out_g[k] for k in names], *[out_d[k] for k in names],
            *[out_m[k] for k in names], *[out_v[k] for k in names])
```
